```python
import jax, jax.numpy as jnp
from jax import lax
import numpy as np

D_MODEL = 1024
BATCH = 8
SEQ = 2048
DEPTH = 1

CHUNK = 64
Q_BLOCK = 128
POOL_WIDTH = D_MODEL // 2
POOL_WINDOWS = (2, 4, 8, 16)
N_POOL_GROUPS = len(POOL_WINDOWS)
POOL_GROUP = POOL_WIDTH // N_POOL_GROUPS
SB_HEAD_DIM = 64
SB_HEADS = (D_MODEL // 2) // SB_HEAD_DIM
SB_WIDTH = SB_HEADS * SB_HEAD_DIM
IN_WIDTH = POOL_WIDTH + 3 * SB_WIDTH
N_BRANCHES = 2
D_FF = 4 * D_MODEL
RMS_EPS = 1e-6

kernel_name = "hybrid_pool_stickbreak_block"


def rmsnorm(x, gain):
    xf = x.astype(jnp.float32)
    y = xf * lax.rsqrt(jnp.mean(xf * xf, axis=-1, keepdims=True) + RMS_EPS)
    return (y * gain.astype(jnp.float32)).astype(x.dtype)


def multiscale_pool(u, w_pool_mix, pool_scale):
    S = u.shape[1]
    uf = u.astype(jnp.float32)
    cs = jnp.concatenate([jnp.zeros_like(uf[:, :1]), jnp.cumsum(uf, axis=1)], axis=1)
    hi = jnp.arange(S) + 1
    outs = []
    for g, w in enumerate(POOL_WINDOWS):
        c0, c1 = g * POOL_GROUP, (g + 1) * POOL_GROUP
        lo = jnp.maximum(hi - w, 0)
        csg = cs[:, :, c0:c1]
        win_sum = csg[:, hi] - csg[:, lo]
        count = (hi - lo).astype(jnp.float32)[None, :, None]
        pooled = win_sum / count - uf[:, :, c0:c1]
        outs.append(jnp.einsum('bsc,cd->bsd', pooled.astype(u.dtype), w_pool_mix[g]))
    y = jnp.concatenate(outs, axis=-1)
    return y * pool_scale


def stick_breaking_attention(q, k, v):
    S = q.shape[2]
    scale = SB_HEAD_DIM ** -0.5
    outs = []
    for blk in range(S // Q_BLOCK):
        t0, t1 = blk * Q_BLOCK, (blk + 1) * Q_BLOCK
        qb, kb, vb = q[:, :, t0:t1], k[:, :, :t1], v[:, :, :t1]
        z = jnp.einsum('bhqd,bhkd->bhqk', qb, kb).astype(jnp.float32) * scale
        mask = jnp.arange(t1)[None, :] < jnp.arange(t0, t1)[:, None]
        log_1mb = jnp.where(mask, jax.nn.log_sigmoid(-z), 0.0)
        tail = lax.cumsum(log_1mb, axis=3, reverse=True) - log_1mb
        attn = jnp.where(mask, jnp.exp(jax.nn.log_sigmoid(z) + tail), 0.0)
        outs.append(jnp.einsum('bhqk,bhkd->bhqd', attn.astype(vb.dtype), vb))
    return jnp.concatenate(outs, axis=2)


def setup_inputs(seed: int = 0) -> dict:
    key = jax.random.key(seed)
    ks = jax.random.split(key, 20)
    f32 = jnp.float32
    nrm = lambda k, shape, fan_in: jax.random.normal(k, shape, f32) * (fan_in ** -0.5)
    gain = lambda k, n: 1.0 + 0.02 * jax.random.normal(k, (n,), f32)
    return {
        "x": jax.random.normal(ks[0], (BATCH, SEQ, D_MODEL), f32),
        "g_pre_mix": gain(ks[1], D_MODEL),
        "w_in": nrm(ks[2], (D_MODEL, IN_WIDTH), D_MODEL),
        "w_pool_mix": nrm(ks[3], (N_POOL_GROUPS, POOL_GROUP, POOL_GROUP), POOL_GROUP),
        "pool_scale": gain(ks[4], POOL_WIDTH),
        "w_br_pool": nrm(ks[5], (POOL_WIDTH, D_MODEL), POOL_WIDTH),
        "w_br_sb": nrm(ks[6], (SB_WIDTH, D_MODEL), SB_WIDTH),
        "w_gate": nrm(ks[7], (D_MODEL, N_BRANCHES * D_MODEL), D_MODEL),
        "b_gate": 0.01 * jax.random.normal(ks[8], (N_BRANCHES * D_MODEL,), f32),
        "w_out": nrm(ks[9], (D_MODEL, D_MODEL), D_MODEL),
        "g_post_mix": gain(ks[10], D_MODEL),
        "g_pre_mlp": gain(ks[11], D_MODEL),
        "w_up": nrm(ks[12], (D_MODEL, D_FF), D_MODEL),
        "w_down": nrm(ks[13], (D_FF, D_MODEL), D_FF),
        "g_post_mlp": gain(ks[14], D_MODEL),
    }


def reference(x, g_pre_mix, w_in, w_pool_mix, pool_scale, w_br_pool, w_br_sb,
              w_gate, b_gate, w_out, g_post_mix, g_pre_mlp, w_up, w_down, g_post_mlp):
    B, S, _ = x.shape
    for _layer in range(DEPTH):
        h = rmsnorm(x, g_pre_mix)
        proj = jnp.einsum('bsd,de->bse', h, w_in)
        u_pool = proj[..., :POOL_WIDTH]
        qkv = proj[..., POOL_WIDTH:].reshape(B, S, 3, SB_HEADS, SB_HEAD_DIM)
        q, k, v = (jnp.transpose(qkv[:, :, i], (0, 2, 1, 3)) for i in range(3))

        y_pool = jnp.einsum('bsc,cd->bsd', multiscale_pool(u_pool, w_pool_mix, pool_scale), w_br_pool)
        o_sb = stick_breaking_attention(q, k, v)
        o_sb = jnp.transpose(o_sb, (0, 2, 1, 3)).reshape(B, S, SB_WIDTH)
        y_sb = jnp.einsum('bsc,cd->bsd', o_sb, w_br_sb)

        gates = jax.nn.sigmoid(jnp.einsum('bsd,de->bse', h, w_gate) + b_gate)
        merged = gates[..., :D_MODEL] * y_pool + gates[..., D_MODEL:] * y_sb
        mix = jnp.einsum('bsd,de->bse', merged, w_out)
        x = x + rmsnorm(mix, g_post_mix)

        h2 = rmsnorm(x, g_pre_mlp)
        a = jnp.square(jax.nn.relu(jnp.einsum('bsd,df->bsf', h2, w_up)))
        ff = jnp.einsum('bsf,fd->bsd', a, w_down)
        x = x + rmsnorm(ff, g_post_mlp)
    return x
```

```python
import functools

import jax
import jax.numpy as jnp
from jax import lax
from jax.experimental import pallas as pl
from jax.experimental.pallas import tpu as pltpu

RMS_EPS = 1e-6
POOL_WINDOWS = (2, 4, 8, 16)
POOL_GROUP = 128
POOL_WIDTH = POOL_GROUP * len(POOL_WINDOWS)
HEAD_DIM = 64
HALO = 16
TILE = 256

VMEM_LIMIT_BYTES = 52 * 1024 * 1024

BF16 = jnp.bfloat16
F32 = jnp.float32


def _rms_scale(x):
    return lax.rsqrt(jnp.mean(x * x, axis=-1, keepdims=True) + RMS_EPS)


def _const_spec(shape):
    zeros = (0,) * len(shape)
    return pl.BlockSpec(shape, lambda *_: zeros, pipeline_mode=pl.Buffered(1))


def _proj_kernel(x_ref, g_pre_ref, w_in_ref, w_pm_ref, pscale_ref, w_brp_ref,
                 w_gate_ref, b_gate_ref,
                 q_ref, k_ref, vt_ref, gy_ref, gsb_ref, uext_ref, *, n_heads, d_model):
    s = pl.program_id(1)
    tm = x_ref.shape[1]
    sb_width = n_heads * HEAD_DIM

    x = x_ref[0]
    h = (x * _rms_scale(x) * g_pre_ref[...]).astype(BF16)
    proj = jnp.dot(h, w_in_ref[...], preferred_element_type=F32)

    q = proj[:, POOL_WIDTH:POOL_WIDTH + sb_width] * (HEAD_DIM ** -0.5)
    k = proj[:, POOL_WIDTH + sb_width:POOL_WIDTH + 2 * sb_width]
    v = proj[:, POOL_WIDTH + 2 * sb_width:POOL_WIDTH + 3 * sb_width]
    for hh in range(n_heads):
        q_ref[0, hh, 0] = q[:, hh * HEAD_DIM:(hh + 1) * HEAD_DIM].astype(BF16)
        k_ref[0, hh, 0] = k[:, hh * HEAD_DIM:(hh + 1) * HEAD_DIM].astype(BF16)
    vt_ref[0, :, 0] = v.T.astype(BF16).reshape(n_heads, HEAD_DIM, tm)

    @pl.when(s == 0)
    def _():
        uext_ref[0:HALO, :] = jnp.zeros((HALO, POOL_WIDTH), F32)

    @pl.when(s > 0)
    def _():
        uext_ref[0:HALO, :] = uext_ref[tm:tm + HALO, :]

    u = proj[:, :POOL_WIDTH]
    uext_ref[HALO:HALO + tm, :] = u

    pos = s * tm + lax.broadcasted_iota(jnp.int32, (tm, POOL_GROUP), 0)
    mixed = []
    for g, w in enumerate(POOL_WINDOWS):
        c0, c1 = g * POOL_GROUP, (g + 1) * POOL_GROUP
        win = uext_ref[HALO:HALO + tm, c0:c1]
        for i in range(1, w):
            win = win + uext_ref[HALO - i:HALO - i + tm, c0:c1]
        count = jnp.minimum(pos + 1, w).astype(F32)
        pooled = win / count - u[:, c0:c1]
        mixed.append(jnp.dot(pooled.astype(BF16), w_pm_ref[g], preferred_element_type=F32))
    y = jnp.concatenate(mixed, axis=-1) * pscale_ref[...]
    y_pool = jnp.dot(y.astype(BF16), w_brp_ref[...], preferred_element_type=F32)

    gates = jax.nn.sigmoid(
        jnp.dot(h, w_gate_ref[...], preferred_element_type=F32) + b_gate_ref[...])
    gy_ref[0] = (gates[:, :d_model] * y_pool).astype(BF16)
    gsb_ref[0] = gates[:, d_model:].astype(BF16)


def _attn_kernel(q_ref, k_ref, vt_ref, ot_ref):
    n_blk, blk, _ = q_ref.shape[2:]
    row = lax.broadcasted_iota(jnp.int32, (blk, blk), 0)
    col = lax.broadcasted_iota(jnp.int32, (blk, blk), 1)
    tri = jnp.where(col > row, 1.0, 0.0).astype(BF16)
    causal = row < col

    def tile(q, j, carry, acc, masked):
        kj = k_ref[0, 0, j]
        zt = lax.dot_general(kj, q, (((1,), (1,)), ((), ())), preferred_element_type=F32)
        soft = jnp.log(1.0 + jnp.exp(-jnp.abs(zt)))
        lsz = jnp.minimum(zt, 0.0) - soft
        l1m = lsz - zt
        if masked:
            l1m = jnp.where(causal, l1m, 0.0)
        after = jnp.dot(tri, l1m.astype(BF16), preferred_element_type=F32)
        a = jnp.exp(lsz + after + carry)
        if masked:
            a = jnp.where(causal, a, 0.0)
        acc = acc + jnp.dot(vt_ref[0, 0, j], a.astype(BF16), preferred_element_type=F32)
        carry = carry + after[0:1, :] + l1m[0:1, :]
        return carry, acc

    def q_block(i, _):
        q = q_ref[0, 0, i]
        carry = jnp.zeros((1, blk), F32)
        acc = jnp.zeros((HEAD_DIM, blk), F32)
        carry, acc = tile(q, i, carry, acc, masked=True)

        def off_diag(it, state):
            return tile(q, i - 1 - it, state[0], state[1], masked=False)

        carry, acc = lax.fori_loop(0, i, off_diag, (carry, acc))
        ot_ref[0, 0, i] = acc.astype(BF16)
        return 0

    lax.fori_loop(0, n_blk, q_block, 0)


def _out_kernel(x_ref, ot_ref, gy_ref, gsb_ref, w_brs_ref, w_out_ref, g_post_mix_ref,
                g_pre_mlp_ref, w_up_ref, w_down_ref, g_post_mlp_ref, o_ref):
    tm = x_ref.shape[1]
    ot = ot_ref[0, :, 0].reshape(-1, tm)
    o_sb = ot.astype(F32).T.astype(BF16)
    y_sb = jnp.dot(o_sb, w_brs_ref[...], preferred_element_type=F32)
    merged = gy_ref[0].astype(F32) + gsb_ref[0].astype(F32) * y_sb
    mix = jnp.dot(merged.astype(BF16), w_out_ref[...], preferred_element_type=F32)
    x1 = x_ref[0] + mix * _rms_scale(mix) * g_post_mix_ref[...]

    h2 = (x1 * _rms_scale(x1) * g_pre_mlp_ref[...]).astype(BF16)
    up = jnp.dot(h2, w_up_ref[...], preferred_element_type=F32)
    act = jnp.square(jnp.maximum(up, 0.0)).astype(BF16)
    ff = jnp.dot(act, w_down_ref[...], preferred_element_type=F32)
    o_ref[0] = x1 + ff * _rms_scale(ff) * g_post_mlp_ref[...]


def kernel(x, g_pre_mix, w_in, w_pool_mix, pool_scale, w_br_pool, w_br_sb, w_gate, b_gate,
           w_out, g_post_mix, g_pre_mlp, w_up, w_down, g_post_mlp):
    B, S, D = x.shape
    in_width = w_in.shape[1]
    sb_width = w_br_sb.shape[0]
    n_heads = sb_width // HEAD_DIM
    d_ff = w_up.shape[1]
    n_groups = len(POOL_WINDOWS)
    assert in_width == POOL_WIDTH + 3 * sb_width and w_br_pool.shape[0] == POOL_WIDTH
    assert S % TILE == 0 and w_gate.shape[1] == 2 * D
    n_blk = S // TILE

    row = lambda a: a.reshape(1, -1).astype(F32)
    bf = lambda a: a.astype(BF16)

    tok_spec = lambda width: pl.BlockSpec((1, TILE, width), lambda b, s: (b, s, 0))
    head_tile = pl.BlockSpec((1, n_heads, 1, TILE, HEAD_DIM), lambda b, s: (b, 0, s, 0, 0))
    head_tile_t = pl.BlockSpec((1, n_heads, 1, HEAD_DIM, TILE), lambda b, s: (b, 0, s, 0, 0))
    q, k, vt, gy, gsb = pl.pallas_call(
        functools.partial(_proj_kernel, n_heads=n_heads, d_model=D),
        grid=(B, n_blk),
        in_specs=[
            tok_spec(D),
            _const_spec((1, D)),
            _const_spec((D, in_width)),
            _const_spec((n_groups, POOL_GROUP, POOL_GROUP)),
            _const_spec((1, POOL_WIDTH)),
            _const_spec((POOL_WIDTH, D)),
            _const_spec((D, 2 * D)),
            _const_spec((1, 2 * D)),
        ],
        out_specs=[head_tile, head_tile, head_tile_t, tok_spec(D), tok_spec(D)],
        out_shape=[
            jax.ShapeDtypeStruct((B, n_heads, n_blk, TILE, HEAD_DIM), BF16),
            jax.ShapeDtypeStruct((B, n_heads, n_blk, TILE, HEAD_DIM), BF16),
            jax.ShapeDtypeStruct((B, n_heads, n_blk, HEAD_DIM, TILE), BF16),
            jax.ShapeDtypeStruct((B, S, D), BF16),
            jax.ShapeDtypeStruct((B, S, D), BF16),
        ],
        scratch_shapes=[pltpu.VMEM((HALO + TILE, POOL_WIDTH), F32)],
        compiler_params=pltpu.CompilerParams(
            dimension_semantics=("arbitrary", "arbitrary"),
            vmem_limit_bytes=VMEM_LIMIT_BYTES),
        name="proj_pool_gates",
    )(x, row(g_pre_mix), bf(w_in), bf(w_pool_mix), row(pool_scale), bf(w_br_pool),
      bf(w_gate), row(b_gate))

    seq_spec = lambda shape: pl.BlockSpec((1, 1) + shape, lambda b, h: (b, h, 0, 0, 0))
    ot = pl.pallas_call(
        _attn_kernel,
        grid=(B, n_heads),
        in_specs=[seq_spec((n_blk, TILE, HEAD_DIM)), seq_spec((n_blk, TILE, HEAD_DIM)),
                  seq_spec((n_blk, HEAD_DIM, TILE))],
        out_specs=seq_spec((n_blk, HEAD_DIM, TILE)),
        out_shape=jax.ShapeDtypeStruct((B, n_heads, n_blk, HEAD_DIM, TILE), BF16),
        compiler_params=pltpu.CompilerParams(
            dimension_semantics=("parallel", "parallel"),
            vmem_limit_bytes=VMEM_LIMIT_BYTES),
        name="stickbreak_attn",
    )(q, k, vt)

    return pl.pallas_call(
        _out_kernel,
        grid=(B, n_blk),
        in_specs=[
            tok_spec(D),
            head_tile_t,
            tok_spec(D),
            tok_spec(D),
            _const_spec((sb_width, D)),
            _const_spec((D, D)),
            _const_spec((1, D)),
            _const_spec((1, D)),
            _const_spec((D, d_ff)),
            _const_spec((d_ff, D)),
            _const_spec((1, D)),
        ],
        out_specs=tok_spec(D),
        out_shape=jax.ShapeDtypeStruct((B, S, D), x.dtype),
        compiler_params=pltpu.CompilerParams(
            dimension_semantics=("parallel", "parallel"),
            vmem_limit_bytes=VMEM_LIMIT_BYTES),
        name="merge_mlp",
    )(x, ot, gy, gsb, bf(w_br_sb), bf(w_out), row(g_post_mix), row(g_pre_mlp),
      bf(w_up), bf(w_down), row(g_post_mlp))
```

```python
import functools

import jax
import jax.numpy as jnp
from jax import lax
from jax.experimental import pallas as pl
from jax.experimental.pallas import tpu as pltpu

RMS_EPS = 1e-6
POOL_WINDOWS = (2, 4, 8, 16)
POOL_GROUP = 128
POOL_WIDTH = POOL_GROUP * len(POOL_WINDOWS)
HEAD_DIM = 64
HALO = 16
TILE = 256
DEAD_LOG = -105.0

VMEM_LIMIT_BYTES = 52 * 1024 * 1024

BF16 = jnp.bfloat16
F32 = jnp.float32


def _rms_scale(x):
    return lax.rsqrt(jnp.mean(x * x, axis=-1, keepdims=True) + RMS_EPS)


def _const_spec(shape):
    zeros = (0,) * len(shape)
    return pl.BlockSpec(shape, lambda *_: zeros, pipeline_mode=pl.Buffered(1))


def _proj_kernel(x_ref, g_pre_ref, w_in_ref, w_pm_ref, pscale_ref, w_brp_ref,
                 w_gate_ref, b_gate_ref,
                 q_ref, k_ref, v_ref, gy_ref, gsb_ref, uext_ref, *, n_heads, d_model):
    s = pl.program_id(1)
    tm = x_ref.shape[1]
    sb_width = n_heads * HEAD_DIM

    x = x_ref[0]
    h = (x * _rms_scale(x) * g_pre_ref[...]).astype(BF16)
    proj = jnp.dot(h, w_in_ref[...], preferred_element_type=F32)

    q = proj[:, POOL_WIDTH:POOL_WIDTH + sb_width] * (HEAD_DIM ** -0.5)
    k = proj[:, POOL_WIDTH + sb_width:POOL_WIDTH + 2 * sb_width]
    v = proj[:, POOL_WIDTH + 2 * sb_width:POOL_WIDTH + 3 * sb_width]
    for hh in range(n_heads):
        q_ref[0, hh, 0] = q[:, hh * HEAD_DIM:(hh + 1) * HEAD_DIM].astype(BF16)
        k_ref[0, hh, 0] = k[:, hh * HEAD_DIM:(hh + 1) * HEAD_DIM].astype(BF16)
        v_ref[0, hh, 0] = v[:, hh * HEAD_DIM:(hh + 1) * HEAD_DIM].astype(BF16)

    @pl.when(s == 0)
    def _():
        uext_ref[0:HALO, :] = jnp.zeros((HALO, POOL_WIDTH), F32)

    @pl.when(s > 0)
    def _():
        uext_ref[0:HALO, :] = uext_ref[tm:tm + HALO, :]

    u = proj[:, :POOL_WIDTH]
    uext_ref[HALO:HALO + tm, :] = u

    pos = s * tm + lax.broadcasted_iota(jnp.int32, (tm, POOL_GROUP), 0)
    mixed = []
    for g, w in enumerate(POOL_WINDOWS):
        c0, c1 = g * POOL_GROUP, (g + 1) * POOL_GROUP
        win = uext_ref[HALO:HALO + tm, c0:c1]
        for i in range(1, w):
            win = win + uext_ref[HALO - i:HALO - i + tm, c0:c1]
        count = jnp.minimum(pos + 1, w).astype(F32)
        pooled = win / count - u[:, c0:c1]
        mixed.append(jnp.dot(pooled.astype(BF16), w_pm_ref[g], preferred_element_type=F32))
    y = jnp.concatenate(mixed, axis=-1) * pscale_ref[...]
    y_pool = jnp.dot(y.astype(BF16), w_brp_ref[...], preferred_element_type=F32)

    gates = jax.nn.sigmoid(
        jnp.dot(h, w_gate_ref[...], preferred_element_type=F32) + b_gate_ref[...])
    gy_ref[0] = (gates[:, :d_model] * y_pool).astype(BF16)
    gsb_ref[0] = gates[:, d_model:].astype(BF16)


def _attn_kernel(q_ref, k_ref, v_ref, o_ref):
    i = pl.program_id(1)
    n_heads, _, blk, _ = q_ref.shape[1:]
    row = lax.broadcasted_iota(jnp.int32, (blk, blk), 0)
    col = lax.broadcasted_iota(jnp.int32, (blk, blk), 1)
    tri = jnp.where(row > col, 1.0, 0.0).astype(BF16)
    causal = col < row

    def tile(hh, j, carry, acc, masked):
        q = q_ref[0, hh, 0]
        kj = k_ref[0, hh, j]
        z = lax.dot_general(q, kj, (((1,), (1,)), ((), ())), preferred_element_type=F32)
        soft = jnp.log(1.0 + jnp.exp(-jnp.abs(z)))
        lsz = jnp.minimum(z, 0.0) - soft
        l1m = lsz - z
        if masked:
            l1m = jnp.where(causal, l1m, 0.0)
        after = jnp.dot(l1m.astype(BF16), tri, preferred_element_type=F32)
        a = jnp.exp(lsz + after + carry)
        if masked:
            a = jnp.where(causal, a, 0.0)
        acc = acc + jnp.dot(a.astype(BF16), v_ref[0, hh, j], preferred_element_type=F32)
        carry = carry + after[:, 0:1] + l1m[:, 0:1]
        return carry, acc

    def all_heads(j, state, masked):
        return tuple(tile(hh, j, *state[hh], masked) for hh in range(n_heads))

    def any_alive(state):
        worst = functools.reduce(jnp.maximum, [carry for carry, _ in state])
        return jnp.max(worst) > DEAD_LOG

    def step(loop):
        it, _, state = loop
        state = all_heads(i - 1 - it, state, masked=False)
        return it + 1, any_alive(state), state

    init = tuple((jnp.zeros((blk, 1), F32), jnp.zeros((blk, HEAD_DIM), F32))
                 for _ in range(n_heads))
    state = all_heads(i, init, masked=True)
    _, _, state = lax.while_loop(lambda loop: jnp.logical_and(loop[0] < i, loop[1]),
                                 step, (jnp.int32(0), any_alive(state), state))
    o_ref[0] = jnp.concatenate([acc for _, acc in state], axis=-1).astype(BF16)


def _out_kernel(x_ref, osb_ref, gy_ref, gsb_ref, w_brs_ref, w_out_ref, g_post_mix_ref,
                g_pre_mlp_ref, w_up_ref, w_down_ref, g_post_mlp_ref, o_ref):
    y_sb = jnp.dot(osb_ref[0], w_brs_ref[...], preferred_element_type=F32)
    merged = gy_ref[0].astype(F32) + gsb_ref[0].astype(F32) * y_sb
    mix = jnp.dot(merged.astype(BF16), w_out_ref[...], preferred_element_type=F32)
    x1 = x_ref[0] + mix * _rms_scale(mix) * g_post_mix_ref[...]

    h2 = (x1 * _rms_scale(x1) * g_pre_mlp_ref[...]).astype(BF16)
    up = jnp.dot(h2, w_up_ref[...], preferred_element_type=F32)
    act = jnp.square(jnp.maximum(up, 0.0)).astype(BF16)
    ff = jnp.dot(act, w_down_ref[...], preferred_element_type=F32)
    o_ref[0] = x1 + ff * _rms_scale(ff) * g_post_mlp_ref[...]


def kernel(x, g_pre_mix, w_in, w_pool_mix, pool_scale, w_br_pool, w_br_sb, w_gate, b_gate,
           w_out, g_post_mix, g_pre_mlp, w_up, w_down, g_post_mlp):
    B, S, D = x.shape
    in_width = w_in.shape[1]
    sb_width = w_br_sb.shape[0]
    n_heads = sb_width // HEAD_DIM
    d_ff = w_up.shape[1]
    n_groups = len(POOL_WINDOWS)
    assert in_width == POOL_WIDTH + 3 * sb_width and w_br_pool.shape[0] == POOL_WIDTH
    assert S % TILE == 0 and w_gate.shape[1] == 2 * D
    n_blk = S // TILE

    row = lambda a: a.reshape(1, -1).astype(F32)
    bf = lambda a: a.astype(BF16)

    tok_spec = lambda width: pl.BlockSpec((1, TILE, width), lambda b, s: (b, s, 0))
    head_tile = pl.BlockSpec((1, n_heads, 1, TILE, HEAD_DIM), lambda b, s: (b, 0, s, 0, 0))
    head_shape = jax.ShapeDtypeStruct((B, n_heads, n_blk, TILE, HEAD_DIM), BF16)
    q, k, v, gy, gsb = pl.pallas_call(
        functools.partial(_proj_kernel, n_heads=n_heads, d_model=D),
        grid=(B, n_blk),
        in_specs=[
            tok_spec(D),
            _const_spec((1, D)),
            _const_spec((D, in_width)),
            _const_spec((n_groups, POOL_GROUP, POOL_GROUP)),
            _const_spec((1, POOL_WIDTH)),
            _const_spec((POOL_WIDTH, D)),
            _const_spec((D, 2 * D)),
            _const_spec((1, 2 * D)),
        ],
        out_specs=[head_tile, head_tile, head_tile, tok_spec(D), tok_spec(D)],
        out_shape=[head_shape, head_shape, head_shape,
                   jax.ShapeDtypeStruct((B, S, D), BF16),
                   jax.ShapeDtypeStruct((B, S, D), BF16)],
        scratch_shapes=[pltpu.VMEM((HALO + TILE, POOL_WIDTH), F32)],
        compiler_params=pltpu.CompilerParams(
            dimension_semantics=("arbitrary", "arbitrary"),
            vmem_limit_bytes=VMEM_LIMIT_BYTES),
        name="proj_pool_gates",
    )(x, row(g_pre_mix), bf(w_in), bf(w_pool_mix), row(pool_scale), bf(w_br_pool),
      bf(w_gate), row(b_gate))

    seq_spec = pl.BlockSpec((1, n_heads, n_blk, TILE, HEAD_DIM), lambda b, i: (b, 0, 0, 0, 0))
    o_sb = pl.pallas_call(
        _attn_kernel,
        grid=(B, n_blk),
        in_specs=[head_tile, seq_spec, seq_spec],
        out_specs=tok_spec(sb_width),
        out_shape=jax.ShapeDtypeStruct((B, S, sb_width), BF16),
        compiler_params=pltpu.CompilerParams(
            dimension_semantics=("parallel", "parallel"),
            vmem_limit_bytes=VMEM_LIMIT_BYTES),
        name="stickbreak_attn",
    )(q, k, v)

    return pl.pallas_call(
        _out_kernel,
        grid=(B, n_blk),
        in_specs=[
            tok_spec(D),
            tok_spec(sb_width),
            tok_spec(D),
            tok_spec(D),
            _const_spec((sb_width, D)),
            _const_spec((D, D)),
            _const_spec((1, D)),
            _const_spec((1, D)),
            _const_spec((D, d_ff)),
            _const_spec((d_ff, D)),
            _const_spec((1, D)),
        ],
        out_specs=tok_spec(D),
        out_shape=jax.ShapeDtypeStruct((B, S, D), x.dtype),
        compiler_params=pltpu.CompilerParams(
            dimension_semantics=("parallel", "parallel"),
            vmem_limit_bytes=VMEM_LIMIT_BYTES),
        name="merge_mlp",
    )(x, o_sb, gy, gsb, bf(w_br_sb), bf(w_out), row(g_post_mix), row(g_pre_mlp),
      bf(w_up), bf(w_down), row(g_post_mlp))
```

```python
import functools

import jax
import jax.numpy as jnp
from jax import lax
from jax.experimental import pallas as pl
from jax.experimental.pallas import tpu as pltpu

RMS_EPS = 1e-6
POOL_WINDOWS = (2, 4, 8, 16)
POOL_GROUP = 128
POOL_WIDTH = POOL_GROUP * len(POOL_WINDOWS)
HEAD_DIM = 64
HALO = 16
TILE = 256
DEAD_LOG = -105.0
LOG2E = 1.4426950408889634

VMEM_LIMIT_BYTES = 52 * 1024 * 1024

BF16 = jnp.bfloat16
F32 = jnp.float32


def _rms_scale(x):
    return lax.rsqrt(jnp.mean(x * x, axis=-1, keepdims=True) + RMS_EPS)


def _const_spec(shape):
    zeros = (0,) * len(shape)
    return pl.BlockSpec(shape, lambda *_: zeros, pipeline_mode=pl.Buffered(1))


def _proj_kernel(x_ref, g_pre_ref, w_in_ref, w_pm_ref, pscale_ref, w_brp_ref,
                 w_gate_ref, b_gate_ref,
                 q_ref, k_ref, v_ref, gy_ref, gsb_ref, uext_ref, *, n_heads, d_model):
    s = pl.program_id(1)
    tm = x_ref.shape[1]
    sb_width = n_heads * HEAD_DIM

    x = x_ref[0]
    h = (x * _rms_scale(x) * g_pre_ref[...]).astype(BF16)
    proj = jnp.dot(h, w_in_ref[...], preferred_element_type=F32)

    q = proj[:, POOL_WIDTH:POOL_WIDTH + sb_width] * (HEAD_DIM ** -0.5)
    k = proj[:, POOL_WIDTH + sb_width:POOL_WIDTH + 2 * sb_width]
    v = proj[:, POOL_WIDTH + 2 * sb_width:POOL_WIDTH + 3 * sb_width]
    for hh in range(n_heads):
        q_ref[0, hh, 0] = q[:, hh * HEAD_DIM:(hh + 1) * HEAD_DIM].astype(BF16)
        k_ref[0, hh, 0] = k[:, hh * HEAD_DIM:(hh + 1) * HEAD_DIM].astype(BF16)
        v_ref[0, hh, 0] = v[:, hh * HEAD_DIM:(hh + 1) * HEAD_DIM].astype(BF16)

    @pl.when(s == 0)
    def _():
        uext_ref[0:HALO, :] = jnp.zeros((HALO, POOL_WIDTH), F32)

    @pl.when(s > 0)
    def _():
        uext_ref[0:HALO, :] = uext_ref[tm:tm + HALO, :]

    u = proj[:, :POOL_WIDTH]
    uext_ref[HALO:HALO + tm, :] = u

    pos = s * tm + lax.broadcasted_iota(jnp.int32, (tm, POOL_GROUP), 0)
    mixed = []
    for g, w in enumerate(POOL_WINDOWS):
        c0, c1 = g * POOL_GROUP, (g + 1) * POOL_GROUP
        win = uext_ref[HALO:HALO + tm, c0:c1]
        for i in range(1, w):
            win = win + uext_ref[HALO - i:HALO - i + tm, c0:c1]
        count = jnp.minimum(pos + 1, w).astype(F32)
        pooled = win / count - u[:, c0:c1]
        mixed.append(jnp.dot(pooled.astype(BF16), w_pm_ref[g], preferred_element_type=F32))
    y = jnp.concatenate(mixed, axis=-1) * pscale_ref[...]
    y_pool = jnp.dot(y.astype(BF16), w_brp_ref[...], preferred_element_type=F32)

    gates = jax.nn.sigmoid(
        jnp.dot(h, w_gate_ref[...], preferred_element_type=F32) + b_gate_ref[...])
    gy_ref[0] = (gates[:, :d_model] * y_pool).astype(BF16)
    gsb_ref[0] = gates[:, d_model:].astype(BF16)


def _attn_kernel(q_ref, k_ref, v_ref, o_ref, carry_ref, acc_ref):
    i = pl.program_id(1)
    n_heads, _, blk, _ = q_ref.shape[1:]
    wide = 2 * blk

    def strict_lower(n):
        r = lax.broadcasted_iota(jnp.int32, (n, n), 0)
        c = lax.broadcasted_iota(jnp.int32, (n, n), 1)
        return jnp.where(r > c, 1.0, 0.0).astype(BF16)

    def scores(q, keys):
        return lax.dot_general(q, keys, (((1,), (1,)), ((), ())), preferred_element_type=F32)

    def log_terms(z):
        soft = jnp.log(1.0 + jnp.exp2(jnp.abs(z) * -LOG2E))
        lsz = jnp.minimum(z, 0.0) - soft
        return lsz, lsz - z

    prev = jnp.maximum(i - 1, 0)
    row = lax.broadcasted_iota(jnp.int32, (blk, wide), 0)
    col = lax.broadcasted_iota(jnp.int32, (blk, wide), 1)
    causal = col < row + blk
    lsz_all, l1m_all = [], []
    for hh in range(n_heads):
        keys = jnp.concatenate([k_ref[0, hh, prev], k_ref[0, hh, i]], axis=0)
        lsz, l1m = log_terms(scores(q_ref[0, hh, 0], keys))
        lsz_all.append(lsz)
        l1m_all.append(jnp.where(causal, l1m, 0.0))
    after_all = jnp.dot(jnp.concatenate(l1m_all, axis=0).astype(BF16), strict_lower(wide),
                        preferred_element_type=F32)
    state = []
    for hh in range(n_heads):
        after = after_all[hh * blk:(hh + 1) * blk]
        a = jnp.where(causal, jnp.exp(lsz_all[hh] + after), 0.0)
        v_prev = jnp.where(i > 0, v_ref[0, hh, prev], jnp.zeros((), BF16))
        vals = jnp.concatenate([v_prev, v_ref[0, hh, i]], axis=0)
        acc = jnp.dot(a.astype(BF16), vals, preferred_element_type=F32)
        state.append((after[:, 0:1] + l1m_all[hh][:, 0:1], acc))
    o_ref[0] = jnp.concatenate([acc for _, acc in state], axis=-1).astype(BF16)

    def any_alive(carries):
        return jnp.max(functools.reduce(jnp.maximum, carries)) > DEAD_LOG

    @pl.when(jnp.logical_and(i > 1, any_alive([carry for carry, _ in state])))
    def _():
        for hh, (carry, acc) in enumerate(state):
            carry_ref[hh] = carry
            acc_ref[hh] = acc
        tri = strict_lower(blk)

        def step(loop):
            it, _ = loop
            j = i - 2 - it
            carries = []
            for hh in range(n_heads):
                lsz, l1m = log_terms(scores(q_ref[0, hh, 0], k_ref[0, hh, j]))
                after = jnp.dot(l1m.astype(BF16), tri, preferred_element_type=F32)
                a = jnp.exp(lsz + after + carry_ref[hh])
                acc_ref[hh] += jnp.dot(a.astype(BF16), v_ref[0, hh, j],
                                       preferred_element_type=F32)
                carries.append(carry_ref[hh] + after[:, 0:1] + l1m[:, 0:1])
                carry_ref[hh] = carries[-1]
            return it + 1, any_alive(carries)

        lax.while_loop(lambda loop: jnp.logical_and(loop[0] < i - 1, loop[1]),
                       step, (jnp.int32(0), True))
        o_ref[0] = jnp.concatenate([acc_ref[hh] for hh in range(n_heads)],
                                   axis=-1).astype(BF16)


def _out_kernel(x_ref, osb_ref, gy_ref, gsb_ref, w_brs_ref, w_out_ref, g_post_mix_ref,
                g_pre_mlp_ref, w_up_ref, w_down_ref, g_post_mlp_ref, o_ref):
    y_sb = jnp.dot(osb_ref[0], w_brs_ref[...], preferred_element_type=F32)
    merged = gy_ref[0].astype(F32) + gsb_ref[0].astype(F32) * y_sb
    mix = jnp.dot(merged.astype(BF16), w_out_ref[...], preferred_element_type=F32)
    x1 = x_ref[0] + mix * _rms_scale(mix) * g_post_mix_ref[...]

    h2 = (x1 * _rms_scale(x1) * g_pre_mlp_ref[...]).astype(BF16)
    up = jnp.dot(h2, w_up_ref[...], preferred_element_type=F32)
    act = jnp.square(jnp.maximum(up, 0.0)).astype(BF16)
    ff = jnp.dot(act, w_down_ref[...], preferred_element_type=F32)
    o_ref[0] = x1 + ff * _rms_scale(ff) * g_post_mlp_ref[...]


def kernel(x, g_pre_mix, w_in, w_pool_mix, pool_scale, w_br_pool, w_br_sb, w_gate, b_gate,
           w_out, g_post_mix, g_pre_mlp, w_up, w_down, g_post_mlp):
    B, S, D = x.shape
    in_width = w_in.shape[1]
    sb_width = w_br_sb.shape[0]
    n_heads = sb_width // HEAD_DIM
    d_ff = w_up.shape[1]
    n_groups = len(POOL_WINDOWS)
    assert in_width == POOL_WIDTH + 3 * sb_width and w_br_pool.shape[0] == POOL_WIDTH
    assert S % TILE == 0 and w_gate.shape[1] == 2 * D
    n_blk = S // TILE

    row = lambda a: a.reshape(1, -1).astype(F32)
    bf = lambda a: a.astype(BF16)

    tok_spec = lambda width: pl.BlockSpec((1, TILE, width), lambda b, s: (b, s, 0))
    head_tile = pl.BlockSpec((1, n_heads, 1, TILE, HEAD_DIM), lambda b, s: (b, 0, s, 0, 0))
    head_shape = jax.ShapeDtypeStruct((B, n_heads, n_blk, TILE, HEAD_DIM), BF16)
    q, k, v, gy, gsb = pl.pallas_call(
        functools.partial(_proj_kernel, n_heads=n_heads, d_model=D),
        grid=(B, n_blk),
        in_specs=[
            tok_spec(D),
            _const_spec((1, D)),
            _const_spec((D, in_width)),
            _const_spec((n_groups, POOL_GROUP, POOL_GROUP)),
            _const_spec((1, POOL_WIDTH)),
            _const_spec((POOL_WIDTH, D)),
            _const_spec((D, 2 * D)),
            _const_spec((1, 2 * D)),
        ],
        out_specs=[head_tile, head_tile, head_tile, tok_spec(D), tok_spec(D)],
        out_shape=[head_shape, head_shape, head_shape,
                   jax.ShapeDtypeStruct((B, S, D), BF16),
                   jax.ShapeDtypeStruct((B, S, D), BF16)],
        scratch_shapes=[pltpu.VMEM((HALO + TILE, POOL_WIDTH), F32)],
        compiler_params=pltpu.CompilerParams(
            dimension_semantics=("arbitrary", "arbitrary"),
            vmem_limit_bytes=VMEM_LIMIT_BYTES),
        name="proj_pool_gates",
    )(x, row(g_pre_mix), bf(w_in), bf(w_pool_mix), row(pool_scale), bf(w_br_pool),
      bf(w_gate), row(b_gate))

    seq_spec = pl.BlockSpec((1, n_heads, n_blk, TILE, HEAD_DIM), lambda b, i: (b, 0, 0, 0, 0))
    o_sb = pl.pallas_call(
        _attn_kernel,
        grid=(B, n_blk),
        in_specs=[head_tile, seq_spec, seq_spec],
        out_specs=tok_spec(sb_width),
        out_shape=jax.ShapeDtypeStruct((B, S, sb_width), BF16),
        scratch_shapes=[pltpu.VMEM((n_heads, TILE, 1), F32),
                        pltpu.VMEM((n_heads, TILE, HEAD_DIM), F32)],
        compiler_params=pltpu.CompilerParams(
            dimension_semantics=("parallel", "parallel"),
            vmem_limit_bytes=VMEM_LIMIT_BYTES),
        name="stickbreak_attn",
    )(q, k, v)

    return pl.pallas_call(
        _out_kernel,
        grid=(B, n_blk),
        in_specs=[
            tok_spec(D),
            tok_spec(sb_width),
            tok_spec(D),
            tok_spec(D),
            _const_spec((sb_width, D)),
            _const_spec((D, D)),
            _const_spec((1, D)),
            _const_spec((1, D)),
            _const_spec((D, d_ff)),
            _const_spec((d_ff, D)),
            _const_spec((1, D)),
        ],
        out_specs=tok_spec(D),
        out_shape=jax.ShapeDtypeStruct((B, S, D), x.dtype),
        compiler_params=pltpu.CompilerParams(
            dimension_semantics=("parallel", "parallel"),
            vmem_limit_bytes=VMEM_LIMIT_BYTES),
        name="merge_mlp",
    )(x, o_sb, gy, gsb, bf(w_br_sb), bf(w_out), row(g_post_mix), row(g_pre_mlp),
      bf(w_up), bf(w_down), row(g_post_mlp))
```

```python
import functools

import jax
import jax.numpy as jnp
from jax import lax
from jax.experimental import pallas as pl
from jax.experimental.pallas import tpu as pltpu

RMS_EPS = 1e-6
POOL_WINDOWS = (2, 4, 8, 16)
POOL_GROUP = 128
POOL_WIDTH = POOL_GROUP * len(POOL_WINDOWS)
HEAD_DIM = 64
HALO = 16
TOK_TILE = 512
ATT_BLK = 256
FF_CHUNK = 1024
DEAD_LOG = -105.0
LOG2E = 1.4426950408889634

VMEM_LIMIT_BYTES = 52 * 1024 * 1024

BF16 = jnp.bfloat16
F32 = jnp.float32


def _rms_scale(x):
    return lax.rsqrt(jnp.mean(x * x, axis=-1, keepdims=True) + RMS_EPS)


def _const_spec(shape):
    zeros = (0,) * len(shape)
    return pl.BlockSpec(shape, lambda *_: zeros, pipeline_mode=pl.Buffered(1))


def _proj_kernel(x_ref, g_pre_ref, w_in_ref, w_pm_ref, pscale_ref, w_brp_ref,
                 w_gate_ref, b_gate_ref,
                 q_ref, k_ref, v_ref, gy_ref, gsb_ref, uext_ref, *, n_heads, d_model):
    s = pl.program_id(1)
    tm = x_ref.shape[1]
    sb_width = n_heads * HEAD_DIM

    x = x_ref[0]
    h = (x * _rms_scale(x) * g_pre_ref[...]).astype(BF16)
    proj = jnp.dot(h, w_in_ref[...], preferred_element_type=F32)

    q = proj[:, POOL_WIDTH:POOL_WIDTH + sb_width] * (HEAD_DIM ** -0.5)
    k = proj[:, POOL_WIDTH + sb_width:POOL_WIDTH + 2 * sb_width]
    v = proj[:, POOL_WIDTH + 2 * sb_width:POOL_WIDTH + 3 * sb_width]
    for src, dst in ((q, q_ref), (k, k_ref), (v, v_ref)):
        for hh in range(n_heads):
            cols = src[:, hh * HEAD_DIM:(hh + 1) * HEAD_DIM].astype(BF16)
            for t in range(tm // ATT_BLK):
                dst[0, hh, t] = cols[t * ATT_BLK:(t + 1) * ATT_BLK]

    @pl.when(s == 0)
    def _():
        uext_ref[0:HALO, :] = jnp.zeros((HALO, POOL_WIDTH), F32)

    @pl.when(s > 0)
    def _():
        uext_ref[0:HALO, :] = uext_ref[tm:tm + HALO, :]

    u = proj[:, :POOL_WIDTH]
    uext_ref[HALO:HALO + tm, :] = u

    pos = s * tm + lax.broadcasted_iota(jnp.int32, (tm, POOL_GROUP), 0)
    mixed = []
    for g, w in enumerate(POOL_WINDOWS):
        c0, c1 = g * POOL_GROUP, (g + 1) * POOL_GROUP
        win = uext_ref[HALO:HALO + tm, c0:c1]
        for i in range(1, w):
            win = win + uext_ref[HALO - i:HALO - i + tm, c0:c1]
        count = jnp.minimum(pos + 1, w).astype(F32)
        pooled = win / count - u[:, c0:c1]
        mixed.append(jnp.dot(pooled.astype(BF16), w_pm_ref[g], preferred_element_type=F32))
    y = jnp.concatenate(mixed, axis=-1) * pscale_ref[...]
    y_pool = jnp.dot(y.astype(BF16), w_brp_ref[...], preferred_element_type=F32)

    gates = jax.nn.sigmoid(
        jnp.dot(h, w_gate_ref[...], preferred_element_type=F32) + b_gate_ref[...])
    gy_ref[0] = (gates[:, :d_model] * y_pool).astype(BF16)
    gsb_ref[0] = gates[:, d_model:].astype(BF16)


def _attn_kernel(q_ref, k_ref, v_ref, o_ref, carry_ref, acc_ref):
    i = pl.program_id(1)
    n_heads, _, blk, _ = q_ref.shape[1:]
    wide = 2 * blk

    def strict_lower(n):
        r = lax.broadcasted_iota(jnp.int32, (n, n), 0)
        c = lax.broadcasted_iota(jnp.int32, (n, n), 1)
        return jnp.where(r > c, 1.0, 0.0).astype(BF16)

    def scores(q, keys):
        return lax.dot_general(q, keys, (((1,), (1,)), ((), ())), preferred_element_type=F32)

    def log_terms(z):
        soft = jnp.log(1.0 + jnp.exp2(jnp.abs(z) * -LOG2E))
        lsz = jnp.minimum(z, 0.0) - soft
        return lsz, lsz - z

    prev = jnp.maximum(i - 1, 0)
    row = lax.broadcasted_iota(jnp.int32, (blk, wide), 0)
    col = lax.broadcasted_iota(jnp.int32, (blk, wide), 1)
    causal = col < row + blk
    lsz_all, l1m_all = [], []
    for hh in range(n_heads):
        keys = jnp.concatenate([k_ref[0, hh, prev], k_ref[0, hh, i]], axis=0)
        lsz, l1m = log_terms(scores(q_ref[0, hh, 0], keys))
        lsz_all.append(lsz)
        l1m_all.append(jnp.where(causal, l1m, 0.0))
    after_all = jnp.dot(jnp.concatenate(l1m_all, axis=0).astype(BF16), strict_lower(wide),
                        preferred_element_type=F32)
    state = []
    for hh in range(n_heads):
        after = after_all[hh * blk:(hh + 1) * blk]
        a = jnp.where(causal, jnp.exp(lsz_all[hh] + after), 0.0)
        v_prev = jnp.where(i > 0, v_ref[0, hh, prev], jnp.zeros((), BF16))
        vals = jnp.concatenate([v_prev, v_ref[0, hh, i]], axis=0)
        acc = jnp.dot(a.astype(BF16), vals, preferred_element_type=F32)
        state.append((after[:, 0:1] + l1m_all[hh][:, 0:1], acc))
    o_ref[0] = jnp.concatenate([acc for _, acc in state], axis=-1).astype(BF16)

    def any_alive(carries):
        return jnp.max(functools.reduce(jnp.maximum, carries)) > DEAD_LOG

    @pl.when(jnp.logical_and(i > 1, any_alive([carry for carry, _ in state])))
    def _():
        for hh, (carry, acc) in enumerate(state):
            carry_ref[hh] = carry
            acc_ref[hh] = acc
        tri = strict_lower(blk)

        def step(loop):
            it, _ = loop
            j = i - 2 - it
            carries = []
            for hh in range(n_heads):
                lsz, l1m = log_terms(scores(q_ref[0, hh, 0], k_ref[0, hh, j]))
                after = jnp.dot(l1m.astype(BF16), tri, preferred_element_type=F32)
                a = jnp.exp(lsz + after + carry_ref[hh])
                acc_ref[hh] += jnp.dot(a.astype(BF16), v_ref[0, hh, j],
                                       preferred_element_type=F32)
                carries.append(carry_ref[hh] + after[:, 0:1] + l1m[:, 0:1])
                carry_ref[hh] = carries[-1]
            return it + 1, any_alive(carries)

        lax.while_loop(lambda loop: jnp.logical_and(loop[0] < i - 1, loop[1]),
                       step, (jnp.int32(0), True))
        o_ref[0] = jnp.concatenate([acc_ref[hh] for hh in range(n_heads)],
                                   axis=-1).astype(BF16)


def _out_kernel(x_ref, osb_ref, gy_ref, gsb_ref, w_brs_ref, w_out_ref, g_post_mix_ref,
                g_pre_mlp_ref, w_up_ref, w_down_ref, g_post_mlp_ref, o_ref):
    y_sb = jnp.dot(osb_ref[0], w_brs_ref[...], preferred_element_type=F32)
    merged = gy_ref[0].astype(F32) + gsb_ref[0].astype(F32) * y_sb
    mix = jnp.dot(merged.astype(BF16), w_out_ref[...], preferred_element_type=F32)
    x1 = x_ref[0] + mix * _rms_scale(mix) * g_post_mix_ref[...]

    h2 = (x1 * _rms_scale(x1) * g_pre_mlp_ref[...]).astype(BF16)
    ff = None
    for c0 in range(0, w_up_ref.shape[1], FF_CHUNK):
        up = jnp.dot(h2, w_up_ref[:, c0:c0 + FF_CHUNK], preferred_element_type=F32)
        act = jnp.square(jnp.maximum(up, 0.0)).astype(BF16)
        part = jnp.dot(act, w_down_ref[c0:c0 + FF_CHUNK, :], preferred_element_type=F32)
        ff = part if ff is None else ff + part
    o_ref[0] = x1 + ff * _rms_scale(ff) * g_post_mlp_ref[...]


def _as_row(a):
    return a.reshape(1, -1).astype(F32)


def _tok_spec(width):
    return pl.BlockSpec((1, TOK_TILE, width), lambda b, s: (b, s, 0))


def _project(x, g_pre_mix, w_in, w_pool_mix, pool_scale, w_br_pool, w_gate, b_gate, n_heads):
    B, S, D = x.shape
    sub = TOK_TILE // ATT_BLK
    head_tile = pl.BlockSpec((1, n_heads, sub, ATT_BLK, HEAD_DIM), lambda b, s: (b, 0, s, 0, 0))
    head_shape = jax.ShapeDtypeStruct((B, n_heads, S // ATT_BLK, ATT_BLK, HEAD_DIM), BF16)
    return pl.pallas_call(
        functools.partial(_proj_kernel, n_heads=n_heads, d_model=D),
        grid=(B, S // TOK_TILE),
        in_specs=[
            _tok_spec(D),
            _const_spec((1, D)),
            _const_spec(w_in.shape),
            _const_spec(w_pool_mix.shape),
            _const_spec((1, POOL_WIDTH)),
            _const_spec(w_br_pool.shape),
            _const_spec(w_gate.shape),
            _const_spec((1, 2 * D)),
        ],
        out_specs=[head_tile, head_tile, head_tile, _tok_spec(D), _tok_spec(D)],
        out_shape=[head_shape, head_shape, head_shape,
                   jax.ShapeDtypeStruct((B, S, D), BF16),
                   jax.ShapeDtypeStruct((B, S, D), BF16)],
        scratch_shapes=[pltpu.VMEM((HALO + TOK_TILE, POOL_WIDTH), F32)],
        compiler_params=pltpu.CompilerParams(
            dimension_semantics=("arbitrary", "arbitrary"),
            vmem_limit_bytes=VMEM_LIMIT_BYTES),
        name="proj_pool_gates",
    )(x, _as_row(g_pre_mix), w_in.astype(BF16), w_pool_mix.astype(BF16), _as_row(pool_scale),
      w_br_pool.astype(BF16), w_gate.astype(BF16), _as_row(b_gate))


def _attention(q, k, v):
    B, n_heads, n_blk = q.shape[:3]
    q_spec = pl.BlockSpec((1, n_heads, 1, ATT_BLK, HEAD_DIM), lambda b, i: (b, 0, i, 0, 0))
    seq_spec = pl.BlockSpec((1, n_heads, n_blk, ATT_BLK, HEAD_DIM), lambda b, i: (b, 0, 0, 0, 0))
    sb_width = n_heads * HEAD_DIM
    return pl.pallas_call(
        _attn_kernel,
        grid=(B, n_blk),
        in_specs=[q_spec, seq_spec, seq_spec],
        out_specs=pl.BlockSpec((1, ATT_BLK, sb_width), lambda b, i: (b, i, 0)),
        out_shape=jax.ShapeDtypeStruct((B, n_blk * ATT_BLK, sb_width), BF16),
        scratch_shapes=[pltpu.VMEM((n_heads, ATT_BLK, 1), F32),
                        pltpu.VMEM((n_heads, ATT_BLK, HEAD_DIM), F32)],
        compiler_params=pltpu.CompilerParams(
            dimension_semantics=("parallel", "parallel"),
            vmem_limit_bytes=VMEM_LIMIT_BYTES),
        name="stickbreak_attn",
    )(q, k, v)


def _merge_mlp(x, o_sb, gy, gsb, w_br_sb, w_out, g_post_mix, g_pre_mlp, w_up, w_down,
               g_post_mlp):
    B, S, D = x.shape
    return pl.pallas_call(
        _out_kernel,
        grid=(B, S // TOK_TILE),
        in_specs=[
            _tok_spec(D),
            _tok_spec(o_sb.shape[-1]),
            _tok_spec(D),
            _tok_spec(D),
            _const_spec(w_br_sb.shape),
            _const_spec(w_out.shape),
            _const_spec((1, D)),
            _const_spec((1, D)),
            _const_spec(w_up.shape),
            _const_spec(w_down.shape),
            _const_spec((1, D)),
        ],
        out_specs=_tok_spec(D),
        out_shape=jax.ShapeDtypeStruct((B, S, D), x.dtype),
        compiler_params=pltpu.CompilerParams(
            dimension_semantics=("parallel", "parallel"),
            vmem_limit_bytes=VMEM_LIMIT_BYTES),
        name="merge_mlp",
    )(x, o_sb, gy, gsb, w_br_sb.astype(BF16), w_out.astype(BF16), _as_row(g_post_mix),
      _as_row(g_pre_mlp), w_up.astype(BF16), w_down.astype(BF16), _as_row(g_post_mlp))


def kernel(x, g_pre_mix, w_in, w_pool_mix, pool_scale, w_br_pool, w_br_sb, w_gate, b_gate,
           w_out, g_post_mix, g_pre_mlp, w_up, w_down, g_post_mlp):
    B, S, D = x.shape
    sb_width = w_br_sb.shape[0]
    assert w_in.shape[1] == POOL_WIDTH + 3 * sb_width and w_br_pool.shape[0] == POOL_WIDTH
    assert S % TOK_TILE == 0 and TOK_TILE % ATT_BLK == 0 and w_gate.shape[1] == 2 * D
    assert w_up.shape[1] % FF_CHUNK == 0

    q, k, v, gy, gsb = _project(x, g_pre_mix, w_in, w_pool_mix, pool_scale, w_br_pool,
                                w_gate, b_gate, sb_width // HEAD_DIM)
    o_sb = _attention(q, k, v)
    return _merge_mlp(x, o_sb, gy, gsb, w_br_sb, w_out, g_post_mix, g_pre_mlp, w_up, w_down,
                      g_post_mlp)
```

```python
import functools

import jax
import jax.numpy as jnp
from jax import lax
from jax.experimental import pallas as pl
from jax.experimental.pallas import tpu as pltpu

RMS_EPS = 1e-6
POOL_WINDOWS = (2, 4, 8, 16)
POOL_GROUP = 128
POOL_WIDTH = POOL_GROUP * len(POOL_WINDOWS)
HEAD_DIM = 64
HALO = 32
TOK_TILE = 512
ATT_BLK = 256
FF_CHUNK = 1024
DEAD_LOG = -105.0
LOG2E = 1.4426950408889634

VMEM_LIMIT_BYTES = 52 * 1024 * 1024

BF16 = jnp.bfloat16
F32 = jnp.float32


def _rms_scale(x):
    return lax.rsqrt(jnp.mean(x * x, axis=-1, keepdims=True) + RMS_EPS)


def _const_spec(shape):
    zeros = (0,) * len(shape)
    return pl.BlockSpec(shape, lambda *_: zeros, pipeline_mode=pl.Buffered(1))


def _proj_kernel(x_ref, g_pre_ref, w_in_ref, w_pm_ref, pscale_ref, w_brp_ref,
                 w_gate_ref, b_gate_ref,
                 q_ref, k_ref, v_ref, gy_ref, gsb_ref, uext_ref, lvl_ref, *, n_heads, d_model):
    s = pl.program_id(1)
    tm = x_ref.shape[1]
    sb_width = n_heads * HEAD_DIM

    @pl.when(jnp.logical_and(pl.program_id(0) == 0, s == 0))
    def _():
        uext_ref[tm:tm + HALO, :] = jnp.zeros((HALO, POOL_WIDTH), F32)

    x = x_ref[0]
    h = (x * _rms_scale(x) * g_pre_ref[...]).astype(BF16)
    proj = jnp.dot(h, w_in_ref[...], preferred_element_type=F32)
    gates = jax.nn.sigmoid(
        jnp.dot(h, w_gate_ref[...], preferred_element_type=F32) + b_gate_ref[...])
    gsb_ref[0] = gates[:, d_model:].astype(BF16)

    q = proj[:, POOL_WIDTH:POOL_WIDTH + sb_width] * (HEAD_DIM ** -0.5)
    k = proj[:, POOL_WIDTH + sb_width:POOL_WIDTH + 2 * sb_width]
    v = proj[:, POOL_WIDTH + 2 * sb_width:POOL_WIDTH + 3 * sb_width]
    for src, dst in ((q, q_ref), (k, k_ref), (v, v_ref)):
        for hh in range(n_heads):
            cols = src[:, hh * HEAD_DIM:(hh + 1) * HEAD_DIM].astype(BF16)
            for t in range(tm // ATT_BLK):
                dst[0, hh, t] = cols[t * ATT_BLK:(t + 1) * ATT_BLK]

    uext_ref[0:HALO, :] = jnp.where(s > 0, uext_ref[tm:tm + HALO, :], 0.0)

    u = proj[:, :POOL_WIDTH]
    uext_ref[HALO:HALO + tm, :] = u

    pos = s * tm + lax.broadcasted_iota(jnp.int32, (tm, POOL_GROUP), 0)
    rows = HALO + tm
    mixed = []
    for g, w in enumerate(POOL_WINDOWS):
        c0, c1 = g * POOL_GROUP, (g + 1) * POOL_GROUP
        levels = w.bit_length() - 1
        start = HALO - 8 * (levels - 1)
        win = uext_ref[start:rows, c0:c1] + uext_ref[start - 1:rows - 1, c0:c1]
        for lvl in range(2, levels + 1):
            back = 1 << (lvl - 1)
            lvl_ref[g, start:rows, :] = win
            start += 8
            win = win[8:] + lvl_ref[g, start - back:rows - back, :]
        count = jnp.minimum(pos + 1, w).astype(F32)
        pooled = win / count - u[:, c0:c1]
        mixed.append(jnp.dot(pooled.astype(BF16), w_pm_ref[g], preferred_element_type=F32))
    y = jnp.concatenate(mixed, axis=-1) * pscale_ref[...]
    y_pool = jnp.dot(y.astype(BF16), w_brp_ref[...], preferred_element_type=F32)

    gy_ref[0] = (gates[:, :d_model] * y_pool).astype(BF16)


def _attn_kernel(q_ref, k_ref, v_ref, o_ref, carry_ref, acc_ref):
    i = pl.program_id(1)
    n_heads, _, blk, _ = q_ref.shape[1:]
    wide = 2 * blk

    def strict_lower(n):
        r = lax.broadcasted_iota(jnp.int32, (n, n), 0)
        c = lax.broadcasted_iota(jnp.int32, (n, n), 1)
        return jnp.where(r > c, 1.0, 0.0).astype(BF16)

    def scores(q, keys):
        return lax.dot_general(q, keys, (((1,), (1,)), ((), ())), preferred_element_type=F32)

    def log_terms(z):
        soft = jnp.log(1.0 + jnp.exp2(jnp.abs(z) * -LOG2E))
        lsz = jnp.minimum(z, 0.0) - soft
        return lsz, lsz - z

    prev = jnp.maximum(i - 1, 0)
    row = lax.broadcasted_iota(jnp.int32, (blk, wide), 0)
    col = lax.broadcasted_iota(jnp.int32, (blk, wide), 1)
    causal = col < row + blk
    lsz_all, l1m_all = [], []
    for hh in range(n_heads):
        keys = jnp.concatenate([k_ref[0, hh, prev], k_ref[0, hh, i]], axis=0)
        lsz, l1m = log_terms(scores(q_ref[0, hh, 0], keys))
        lsz_all.append(lsz)
        l1m_all.append(jnp.where(causal, l1m, 0.0))
    after_all = jnp.dot(jnp.concatenate(l1m_all, axis=0).astype(BF16), strict_lower(wide),
                        preferred_element_type=F32)
    state = []
    for hh in range(n_heads):
        after = after_all[hh * blk:(hh + 1) * blk]
        a = jnp.where(causal, jnp.exp(lsz_all[hh] + after), 0.0)
        v_prev = jnp.where(i > 0, v_ref[0, hh, prev], jnp.zeros((), BF16))
        vals = jnp.concatenate([v_prev, v_ref[0, hh, i]], axis=0)
        acc = jnp.dot(a.astype(BF16), vals, preferred_element_type=F32)
        state.append((after[:, 0:1] + l1m_all[hh][:, 0:1], acc))
    o_ref[0] = jnp.concatenate([acc for _, acc in state], axis=-1).astype(BF16)

    def any_alive(carries):
        return jnp.max(functools.reduce(jnp.maximum, carries)) > DEAD_LOG

    @pl.when(jnp.logical_and(i > 1, any_alive([carry for carry, _ in state])))
    def _():
        for hh, (carry, acc) in enumerate(state):
            carry_ref[hh] = carry
            acc_ref[hh] = acc
        tri = strict_lower(blk)

        def step(loop):
            it, _ = loop
            j = i - 2 - it
            carries = []
            for hh in range(n_heads):
                lsz, l1m = log_terms(scores(q_ref[0, hh, 0], k_ref[0, hh, j]))
                after = jnp.dot(l1m.astype(BF16), tri, preferred_element_type=F32)
                a = jnp.exp(lsz + after + carry_ref[hh])
                acc_ref[hh] += jnp.dot(a.astype(BF16), v_ref[0, hh, j],
                                       preferred_element_type=F32)
                carries.append(carry_ref[hh] + after[:, 0:1] + l1m[:, 0:1])
                carry_ref[hh] = carries[-1]
            return it + 1, any_alive(carries)

        lax.while_loop(lambda loop: jnp.logical_and(loop[0] < i - 1, loop[1]),
                       step, (jnp.int32(0), True))
        o_ref[0] = jnp.concatenate([acc_ref[hh] for hh in range(n_heads)],
                                   axis=-1).astype(BF16)


def _out_kernel(x_ref, osb_ref, gy_ref, gsb_ref, w_brs_ref, w_out_ref, g_post_mix_ref,
                g_pre_mlp_ref, w_up_ref, w_down_ref, g_post_mlp_ref, o_ref):
    y_sb = jnp.dot(osb_ref[0], w_brs_ref[...], preferred_element_type=F32)
    merged = gy_ref[0].astype(F32) + gsb_ref[0].astype(F32) * y_sb
    mix = jnp.dot(merged.astype(BF16), w_out_ref[...], preferred_element_type=F32)
    x1 = x_ref[0] + mix * _rms_scale(mix) * g_post_mix_ref[...]

    h2 = (x1 * _rms_scale(x1) * g_pre_mlp_ref[...]).astype(BF16)
    ff = None
    for c0 in range(0, w_up_ref.shape[1], FF_CHUNK):
        up = jnp.dot(h2, w_up_ref[:, c0:c0 + FF_CHUNK], preferred_element_type=F32)
        act = jnp.square(jnp.maximum(up, 0.0)).astype(BF16)
        part = jnp.dot(act, w_down_ref[c0:c0 + FF_CHUNK, :], preferred_element_type=F32)
        ff = part if ff is None else ff + part
    o_ref[0] = x1 + ff * _rms_scale(ff) * g_post_mlp_ref[...]


def _as_row(a):
    return a.reshape(1, -1).astype(F32)


def _tok_spec(width):
    return pl.BlockSpec((1, TOK_TILE, width), lambda b, s: (b, s, 0))


def _project(x, g_pre_mix, w_in, w_pool_mix, pool_scale, w_br_pool, w_gate, b_gate, n_heads):
    B, S, D = x.shape
    sub = TOK_TILE // ATT_BLK
    head_tile = pl.BlockSpec((1, n_heads, sub, ATT_BLK, HEAD_DIM), lambda b, s: (b, 0, s, 0, 0))
    head_shape = jax.ShapeDtypeStruct((B, n_heads, S // ATT_BLK, ATT_BLK, HEAD_DIM), BF16)
    return pl.pallas_call(
        functools.partial(_proj_kernel, n_heads=n_heads, d_model=D),
        grid=(B, S // TOK_TILE),
        in_specs=[
            _tok_spec(D),
            _const_spec((1, D)),
            _const_spec(w_in.shape),
            _const_spec(w_pool_mix.shape),
            _const_spec((1, POOL_WIDTH)),
            _const_spec(w_br_pool.shape),
            _const_spec(w_gate.shape),
            _const_spec((1, 2 * D)),
        ],
        out_specs=[head_tile, head_tile, head_tile, _tok_spec(D), _tok_spec(D)],
        out_shape=[head_shape, head_shape, head_shape,
                   jax.ShapeDtypeStruct((B, S, D), BF16),
                   jax.ShapeDtypeStruct((B, S, D), BF16)],
        scratch_shapes=[pltpu.VMEM((HALO + TOK_TILE, POOL_WIDTH), F32),
                        pltpu.VMEM((len(POOL_WINDOWS), HALO + TOK_TILE, POOL_GROUP), F32)],
        compiler_params=pltpu.CompilerParams(
            dimension_semantics=("arbitrary", "arbitrary"),
            vmem_limit_bytes=VMEM_LIMIT_BYTES),
        name="proj_pool_gates",
    )(x, _as_row(g_pre_mix), w_in.astype(BF16), w_pool_mix.astype(BF16), _as_row(pool_scale),
      w_br_pool.astype(BF16), w_gate.astype(BF16), _as_row(b_gate))


def _attention(q, k, v):
    B, n_heads, n_blk = q.shape[:3]
    q_spec = pl.BlockSpec((1, n_heads, 1, ATT_BLK, HEAD_DIM), lambda b, i: (b, 0, i, 0, 0))
    seq_spec = pl.BlockSpec((1, n_heads, n_blk, ATT_BLK, HEAD_DIM), lambda b, i: (b, 0, 0, 0, 0))
    sb_width = n_heads * HEAD_DIM
    return pl.pallas_call(
        _attn_kernel,
        grid=(B, n_blk),
        in_specs=[q_spec, seq_spec, seq_spec],
        out_specs=pl.BlockSpec((1, ATT_BLK, sb_width), lambda b, i: (b, i, 0)),
        out_shape=jax.ShapeDtypeStruct((B, n_blk * ATT_BLK, sb_width), BF16),
        scratch_shapes=[pltpu.VMEM((n_heads, ATT_BLK, 1), F32),
                        pltpu.VMEM((n_heads, ATT_BLK, HEAD_DIM), F32)],
        compiler_params=pltpu.CompilerParams(
            dimension_semantics=("parallel", "parallel"),
            vmem_limit_bytes=VMEM_LIMIT_BYTES),
        name="stickbreak_attn",
    )(q, k, v)


def _merge_mlp(x, o_sb, gy, gsb, w_br_sb, w_out, g_post_mix, g_pre_mlp, w_up, w_down,
               g_post_mlp):
    B, S, D = x.shape
    return pl.pallas_call(
        _out_kernel,
        grid=(B, S // TOK_TILE),
        in_specs=[
            _tok_spec(D),
            _tok_spec(o_sb.shape[-1]),
            _tok_spec(D),
            _tok_spec(D),
            _const_spec(w_br_sb.shape),
            _const_spec(w_out.shape),
            _const_spec((1, D)),
            _const_spec((1, D)),
            _const_spec(w_up.shape),
            _const_spec(w_down.shape),
            _const_spec((1, D)),
        ],
        out_specs=_tok_spec(D),
        out_shape=jax.ShapeDtypeStruct((B, S, D), x.dtype),
        compiler_params=pltpu.CompilerParams(
            dimension_semantics=("parallel", "parallel"),
            vmem_limit_bytes=VMEM_LIMIT_BYTES),
        name="merge_mlp",
    )(x, o_sb, gy, gsb, w_br_sb.astype(BF16), w_out.astype(BF16), _as_row(g_post_mix),
      _as_row(g_pre_mlp), w_up.astype(BF16), w_down.astype(BF16), _as_row(g_post_mlp))


def kernel(x, g_pre_mix, w_in, w_pool_mix, pool_scale, w_br_pool, w_br_sb, w_gate, b_gate,
           w_out, g_post_mix, g_pre_mlp, w_up, w_down, g_post_mlp):
    B, S, D = x.shape
    sb_width = w_br_sb.shape[0]
    assert w_in.shape[1] == POOL_WIDTH + 3 * sb_width and w_br_pool.shape[0] == POOL_WIDTH
    assert S % TOK_TILE == 0 and TOK_TILE % ATT_BLK == 0 and w_gate.shape[1] == 2 * D
    assert w_up.shape[1] % FF_CHUNK == 0

    q, k, v, gy, gsb = _project(x, g_pre_mix, w_in, w_pool_mix, pool_scale, w_br_pool,
                                w_gate, b_gate, sb_width // HEAD_DIM)
    o_sb = _attention(q, k, v)
    return _merge_mlp(x, o_sb, gy, gsb, w_br_sb, w_out, g_post_mix, g_pre_mlp, w_up, w_down,
                      g_post_mlp)
```

```python
import functools

import jax
import jax.numpy as jnp
from jax import lax
from jax.experimental import pallas as pl
from jax.experimental.pallas import tpu as pltpu

RMS_EPS = 1e-6
POOL_WINDOWS = (2, 4, 8, 16)
POOL_GROUP = 128
POOL_WIDTH = POOL_GROUP * len(POOL_WINDOWS)
HEAD_DIM = 64
HALO = 32
TOK_TILE = 512
ATT_BLK = 256
FF_CHUNK = 1024
DEAD_LOG2 = -151.0
LOG2E = 1.4426950408889634

VMEM_LIMIT_BYTES = 52 * 1024 * 1024

BF16 = jnp.bfloat16
F32 = jnp.float32


def _rms_scale(x):
    return lax.rsqrt(jnp.mean(x * x, axis=-1, keepdims=True) + RMS_EPS)


def _const_spec(shape):
    zeros = (0,) * len(shape)
    return pl.BlockSpec(shape, lambda *_: zeros, pipeline_mode=pl.Buffered(1))


def _proj_kernel(x_ref, g_pre_ref, w_in_ref, w_pm_ref, pscale_ref, w_brp_ref,
                 w_gate_ref, b_gate_ref,
                 q_ref, k_ref, v_ref, gy_ref, gsb_ref, uext_ref, lvl_ref, *, n_heads, d_model):
    s = pl.program_id(1)
    tm = x_ref.shape[1]
    sb_width = n_heads * HEAD_DIM

    @pl.when(jnp.logical_and(pl.program_id(0) == 0, s == 0))
    def _():
        uext_ref[tm:tm + HALO, :] = jnp.zeros((HALO, POOL_WIDTH), F32)

    x = x_ref[0]
    h = (x * _rms_scale(x) * g_pre_ref[...]).astype(BF16)
    proj = jnp.dot(h, w_in_ref[...], preferred_element_type=F32)
    gates = jax.nn.sigmoid(
        jnp.dot(h, w_gate_ref[...], preferred_element_type=F32) + b_gate_ref[...])
    gsb_ref[0] = gates[:, d_model:].astype(BF16)

    q = proj[:, POOL_WIDTH:POOL_WIDTH + sb_width] * (HEAD_DIM ** -0.5 * LOG2E)
    k = proj[:, POOL_WIDTH + sb_width:POOL_WIDTH + 2 * sb_width]
    v = proj[:, POOL_WIDTH + 2 * sb_width:POOL_WIDTH + 3 * sb_width]
    for src, dst in ((q, q_ref), (k, k_ref), (v, v_ref)):
        for hh in range(n_heads):
            cols = src[:, hh * HEAD_DIM:(hh + 1) * HEAD_DIM].astype(BF16)
            for t in range(tm // ATT_BLK):
                dst[0, hh, t] = cols[t * ATT_BLK:(t + 1) * ATT_BLK]

    uext_ref[0:HALO, :] = jnp.where(s > 0, uext_ref[tm:tm + HALO, :], 0.0)

    u = proj[:, :POOL_WIDTH]
    uext_ref[HALO:HALO + tm, :] = u

    pos = s * tm + lax.broadcasted_iota(jnp.int32, (tm, POOL_GROUP), 0)
    rows = HALO + tm
    mixed = []
    for g, w in enumerate(POOL_WINDOWS):
        c0, c1 = g * POOL_GROUP, (g + 1) * POOL_GROUP
        levels = w.bit_length() - 1
        start = HALO - 8 * (levels - 1)
        win = uext_ref[start:rows, c0:c1] + uext_ref[start - 1:rows - 1, c0:c1]
        for lvl in range(2, levels + 1):
            back = 1 << (lvl - 1)
            lvl_ref[g, start:rows, :] = win
            start += 8
            win = win[8:] + lvl_ref[g, start - back:rows - back, :]
        count = jnp.minimum(pos + 1, w).astype(F32)
        pooled = win / count - u[:, c0:c1]
        mixed.append(jnp.dot(pooled.astype(BF16), w_pm_ref[g], preferred_element_type=F32))
    y = jnp.concatenate(mixed, axis=-1) * pscale_ref[...]
    y_pool = jnp.dot(y.astype(BF16), w_brp_ref[...], preferred_element_type=F32)

    gy_ref[0] = (gates[:, :d_model] * y_pool).astype(BF16)


def _attn_kernel(q_ref, k_ref, v_ref, o_ref, carry_ref, acc_ref):
    i = pl.program_id(1)
    n_heads, _, blk, _ = q_ref.shape[1:]
    wide = 2 * blk

    def strict_lower(n):
        r = lax.broadcasted_iota(jnp.int32, (n, n), 0)
        c = lax.broadcasted_iota(jnp.int32, (n, n), 1)
        return jnp.where(r > c, 1.0, 0.0).astype(BF16)

    def scores(q, keys):
        return lax.dot_general(q, keys, (((1,), (1,)), ((), ())), preferred_element_type=F32)

    def log_terms(z):
        soft = jnp.log(1.0 + jnp.exp2(-jnp.abs(z))) * LOG2E
        lsz = jnp.minimum(z, 0.0) - soft
        return lsz, lsz - z

    prev = jnp.maximum(i - 1, 0)
    row = lax.broadcasted_iota(jnp.int32, (blk, wide), 0)
    col = lax.broadcasted_iota(jnp.int32, (blk, wide), 1)
    causal = col < row + blk
    lsz_all, l1m_all = [], []
    for hh in range(n_heads):
        keys = jnp.concatenate([k_ref[0, hh, prev], k_ref[0, hh, i]], axis=0)
        lsz, l1m = log_terms(scores(q_ref[0, hh, 0], keys))
        lsz_all.append(lsz)
        l1m_all.append(jnp.where(causal, l1m, 0.0))
    tri = strict_lower(blk)
    stack = lambda lo: jnp.concatenate([l1m[:, lo:lo + blk] for l1m in l1m_all],
                                       axis=0).astype(BF16)
    after_prev_all = jnp.dot(stack(0), tri, preferred_element_type=F32)
    after_diag_all = jnp.dot(stack(blk), tri, preferred_element_type=F32)
    state = []
    for hh in range(n_heads):
        l1m = l1m_all[hh]
        after_prev = after_prev_all[hh * blk:(hh + 1) * blk]
        after_diag = after_diag_all[hh * blk:(hh + 1) * blk]
        total_diag = after_diag[:, 0:1] + l1m[:, blk:blk + 1]
        after = jnp.concatenate([after_prev + total_diag, after_diag], axis=1)
        a = jnp.where(causal, jnp.exp2(lsz_all[hh] + after), 0.0)
        v_prev = jnp.where(i > 0, v_ref[0, hh, prev], jnp.zeros((), BF16))
        vals = jnp.concatenate([v_prev, v_ref[0, hh, i]], axis=0)
        acc = jnp.dot(a.astype(BF16), vals, preferred_element_type=F32)
        state.append((after_prev[:, 0:1] + total_diag + l1m[:, 0:1], acc))
    o_ref[0] = jnp.concatenate([acc for _, acc in state], axis=-1).astype(BF16)

    def any_alive(carries):
        return jnp.max(functools.reduce(jnp.maximum, carries)) > DEAD_LOG2

    @pl.when(jnp.logical_and(i > 1, any_alive([carry for carry, _ in state])))
    def _():
        for hh, (carry, acc) in enumerate(state):
            carry_ref[hh] = carry
            acc_ref[hh] = acc

        def step(loop):
            it, _ = loop
            j = i - 2 - it
            carries = []
            for hh in range(n_heads):
                lsz, l1m = log_terms(scores(q_ref[0, hh, 0], k_ref[0, hh, j]))
                after = jnp.dot(l1m.astype(BF16), tri, preferred_element_type=F32)
                a = jnp.exp2(lsz + after + carry_ref[hh])
                acc_ref[hh] += jnp.dot(a.astype(BF16), v_ref[0, hh, j],
                                       preferred_element_type=F32)
                carries.append(carry_ref[hh] + after[:, 0:1] + l1m[:, 0:1])
                carry_ref[hh] = carries[-1]
            return it + 1, any_alive(carries)

        lax.while_loop(lambda loop: jnp.logical_and(loop[0] < i - 1, loop[1]),
                       step, (jnp.int32(0), True))
        o_ref[0] = jnp.concatenate([acc_ref[hh] for hh in range(n_heads)],
                                   axis=-1).astype(BF16)


def _out_kernel(x_ref, osb_ref, gy_ref, gsb_ref, w_brs_ref, w_out_ref, g_post_mix_ref,
                g_pre_mlp_ref, w_up_ref, w_down_ref, g_post_mlp_ref, o_ref):
    def mixed(r):
        y_sb = jnp.dot(osb_ref[0, r], w_brs_ref[...], preferred_element_type=F32)
        merged = gy_ref[0, r].astype(F32) + gsb_ref[0, r].astype(F32) * y_sb
        return jnp.dot(merged.astype(BF16), w_out_ref[...], preferred_element_type=F32)

    def normed(r, mix):
        x1 = x_ref[0, r] + mix * _rms_scale(mix) * g_post_mix_ref[...]
        return x1, (x1 * _rms_scale(x1) * g_pre_mlp_ref[...]).astype(BF16)

    def mlp(h2):
        ff = None
        for c0 in range(0, w_up_ref.shape[1], FF_CHUNK):
            up = jnp.dot(h2, w_up_ref[:, c0:c0 + FF_CHUNK], preferred_element_type=F32)
            act = jnp.square(jnp.maximum(up, 0.0)).astype(BF16)
            part = jnp.dot(act, w_down_ref[c0:c0 + FF_CHUNK, :], preferred_element_type=F32)
            ff = part if ff is None else ff + part
        return ff

    half = x_ref.shape[1] // 2
    halves = (pl.ds(0, half), pl.ds(half, half))
    mixes = [mixed(r) for r in halves]
    normeds = [normed(r, mix) for r, mix in zip(halves, mixes)]
    for r, (x1, h2) in zip(halves, normeds):
        ff = mlp(h2)
        o_ref[0, r] = x1 + ff * _rms_scale(ff) * g_post_mlp_ref[...]


def _as_row(a):
    return a.reshape(1, -1).astype(F32)


def _tok_spec(width):
    return pl.BlockSpec((1, TOK_TILE, width), lambda b, s: (b, s, 0))


def _project(x, g_pre_mix, w_in, w_pool_mix, pool_scale, w_br_pool, w_gate, b_gate, n_heads):
    B, S, D = x.shape
    sub = TOK_TILE // ATT_BLK
    head_tile = pl.BlockSpec((1, n_heads, sub, ATT_BLK, HEAD_DIM), lambda b, s: (b, 0, s, 0, 0))
    head_shape = jax.ShapeDtypeStruct((B, n_heads, S // ATT_BLK, ATT_BLK, HEAD_DIM), BF16)
    return pl.pallas_call(
        functools.partial(_proj_kernel, n_heads=n_heads, d_model=D),
        grid=(B, S // TOK_TILE),
        in_specs=[
            _tok_spec(D),
            _const_spec((1, D)),
            _const_spec(w_in.shape),
            _const_spec(w_pool_mix.shape),
            _const_spec((1, POOL_WIDTH)),
            _const_spec(w_br_pool.shape),
            _const_spec(w_gate.shape),
            _const_spec((1, 2 * D)),
        ],
        out_specs=[head_tile, head_tile, head_tile, _tok_spec(D), _tok_spec(D)],
        out_shape=[head_shape, head_shape, head_shape,
                   jax.ShapeDtypeStruct((B, S, D), BF16),
                   jax.ShapeDtypeStruct((B, S, D), BF16)],
        scratch_shapes=[pltpu.VMEM((HALO + TOK_TILE, POOL_WIDTH), F32),
                        pltpu.VMEM((len(POOL_WINDOWS), HALO + TOK_TILE, POOL_GROUP), F32)],
        compiler_params=pltpu.CompilerParams(
            dimension_semantics=("arbitrary", "arbitrary"),
            vmem_limit_bytes=VMEM_LIMIT_BYTES),
        name="proj_pool_gates",
    )(x, _as_row(g_pre_mix), w_in.astype(BF16), w_pool_mix.astype(BF16), _as_row(pool_scale),
      w_br_pool.astype(BF16), w_gate.astype(BF16), _as_row(b_gate))


def _attention(q, k, v):
    B, n_heads, n_blk = q.shape[:3]
    q_spec = pl.BlockSpec((1, n_heads, 1, ATT_BLK, HEAD_DIM), lambda b, i: (b, 0, i, 0, 0))
    seq_spec = pl.BlockSpec((1, n_heads, n_blk, ATT_BLK, HEAD_DIM), lambda b, i: (b, 0, 0, 0, 0))
    sb_width = n_heads * HEAD_DIM
    return pl.pallas_call(
        _attn_kernel,
        grid=(B, n_blk),
        in_specs=[q_spec, seq_spec, seq_spec],
        out_specs=pl.BlockSpec((1, ATT_BLK, sb_width), lambda b, i: (b, i, 0)),
        out_shape=jax.ShapeDtypeStruct((B, n_blk * ATT_BLK, sb_width), BF16),
        scratch_shapes=[pltpu.VMEM((n_heads, ATT_BLK, 1), F32),
                        pltpu.VMEM((n_heads, ATT_BLK, HEAD_DIM), F32)],
        compiler_params=pltpu.CompilerParams(
            dimension_semantics=("parallel", "parallel"),
            vmem_limit_bytes=VMEM_LIMIT_BYTES),
        name="stickbreak_attn",
    )(q, k, v)


def _merge_mlp(x, o_sb, gy, gsb, w_br_sb, w_out, g_post_mix, g_pre_mlp, w_up, w_down,
               g_post_mlp):
    B, S, D = x.shape
    return pl.pallas_call(
        _out_kernel,
        grid=(B, S // TOK_TILE),
        in_specs=[
            _tok_spec(D),
            _tok_spec(o_sb.shape[-1]),
            _tok_spec(D),
            _tok_spec(D),
            _const_spec(w_br_sb.shape),
            _const_spec(w_out.shape),
            _const_spec((1, D)),
            _const_spec((1, D)),
            _const_spec(w_up.shape),
            _const_spec(w_down.shape),
            _const_spec((1, D)),
        ],
        out_specs=_tok_spec(D),
        out_shape=jax.ShapeDtypeStruct((B, S, D), x.dtype),
        compiler_params=pltpu.CompilerParams(
            dimension_semantics=("parallel", "parallel"),
            vmem_limit_bytes=VMEM_LIMIT_BYTES),
        name="merge_mlp",
    )(x, o_sb, gy, gsb, w_br_sb.astype(BF16), w_out.astype(BF16), _as_row(g_post_mix),
      _as_row(g_pre_mlp), w_up.astype(BF16), w_down.astype(BF16), _as_row(g_post_mlp))


def kernel(x, g_pre_mix, w_in, w_pool_mix, pool_scale, w_br_pool, w_br_sb, w_gate, b_gate,
           w_out, g_post_mix, g_pre_mlp, w_up, w_down, g_post_mlp):
    B, S, D = x.shape
    sb_width = w_br_sb.shape[0]
    assert w_in.shape[1] == POOL_WIDTH + 3 * sb_width and w_br_pool.shape[0] == POOL_WIDTH
    assert S % TOK_TILE == 0 and TOK_TILE % ATT_BLK == 0 and w_gate.shape[1] == 2 * D
    assert w_up.shape[1] % FF_CHUNK == 0

    q, k, v, gy, gsb = _project(x, g_pre_mix, w_in, w_pool_mix, pool_scale, w_br_pool,
                                w_gate, b_gate, sb_width // HEAD_DIM)
    o_sb = _attention(q, k, v)
    return _merge_mlp(x, o_sb, gy, gsb, w_br_sb, w_out, g_post_mix, g_pre_mlp, w_up, w_down,
                      g_post_mlp)
```

```python
import functools

import jax
import jax.numpy as jnp
from jax import lax
from jax.experimental import pallas as pl
from jax.experimental.pallas import tpu as pltpu

RMS_EPS = 1e-6
POOL_WINDOWS = (2, 4, 8, 16)
POOL_GROUP = 128
POOL_WIDTH = POOL_GROUP * len(POOL_WINDOWS)
HEAD_DIM = 64
HALO = 32
TOK_TILE = 512
ATT_BLK = 256
ATT_QBLKS = 2
FF_CHUNK = 1024
DEAD_LOG2 = -151.0
LOG2E = 1.4426950408889634

VMEM_LIMIT_BYTES = 52 * 1024 * 1024

BF16 = jnp.bfloat16
F32 = jnp.float32


def _rms_scale(x):
    return lax.rsqrt(jnp.mean(x * x, axis=-1, keepdims=True) + RMS_EPS)


def _const_spec(shape):
    zeros = (0,) * len(shape)
    return pl.BlockSpec(shape, lambda *_: zeros, pipeline_mode=pl.Buffered(1))


def _proj_kernel(x_ref, g_pre_ref, w_in_ref, w_pm_ref, pscale_ref, w_brp_ref,
                 w_gate_ref, b_gate_ref,
                 q_ref, k_ref, v_ref, gy_ref, gsb_ref, uext_ref, lvl_ref, *, n_heads, d_model):
    s = pl.program_id(1)
    tm = x_ref.shape[1]
    sb_width = n_heads * HEAD_DIM

    @pl.when(jnp.logical_and(pl.program_id(0) == 0, s == 0))
    def _():
        uext_ref[tm:tm + HALO, :] = jnp.zeros((HALO, POOL_WIDTH), F32)

    x = x_ref[0]
    h = (x * _rms_scale(x) * g_pre_ref[...]).astype(BF16)
    proj = jnp.dot(h, w_in_ref[...], preferred_element_type=F32)
    gates = jax.nn.sigmoid(
        jnp.dot(h, w_gate_ref[...], preferred_element_type=F32) + b_gate_ref[...])
    gsb_ref[0] = gates[:, d_model:].astype(BF16)

    q = proj[:, POOL_WIDTH:POOL_WIDTH + sb_width] * (HEAD_DIM ** -0.5 * LOG2E)
    k = proj[:, POOL_WIDTH + sb_width:POOL_WIDTH + 2 * sb_width]
    v = proj[:, POOL_WIDTH + 2 * sb_width:POOL_WIDTH + 3 * sb_width]
    for src, dst in ((q, q_ref), (k, k_ref), (v, v_ref)):
        for hh in range(n_heads):
            cols = src[:, hh * HEAD_DIM:(hh + 1) * HEAD_DIM].astype(BF16)
            for t in range(tm // ATT_BLK):
                dst[0, hh, t] = cols[t * ATT_BLK:(t + 1) * ATT_BLK]

    uext_ref[0:HALO, :] = jnp.where(s > 0, uext_ref[tm:tm + HALO, :], 0.0)

    u = proj[:, :POOL_WIDTH]
    uext_ref[HALO:HALO + tm, :] = u

    pos = s * tm + lax.broadcasted_iota(jnp.int32, (tm, POOL_GROUP), 0)
    rows = HALO + tm
    mixed = []
    for g, w in enumerate(POOL_WINDOWS):
        c0, c1 = g * POOL_GROUP, (g + 1) * POOL_GROUP
        levels = w.bit_length() - 1
        start = HALO - 8 * (levels - 1)
        win = uext_ref[start:rows, c0:c1] + uext_ref[start - 1:rows - 1, c0:c1]
        for lvl in range(2, levels + 1):
            back = 1 << (lvl - 1)
            lvl_ref[g, start:rows, :] = win
            start += 8
            win = win[8:] + lvl_ref[g, start - back:rows - back, :]
        count = jnp.minimum(pos + 1, w).astype(F32)
        pooled = win / count - u[:, c0:c1]
        mixed.append(jnp.dot(pooled.astype(BF16), w_pm_ref[g], preferred_element_type=F32))
    y = jnp.concatenate(mixed, axis=-1) * pscale_ref[...]
    y_pool = jnp.dot(y.astype(BF16), w_brp_ref[...], preferred_element_type=F32)

    gy_ref[0] = (gates[:, :d_model] * y_pool).astype(BF16)


def _attn_kernel(q_ref, k_ref, v_ref, o_ref, carry_ref, acc_ref):
    n_heads, n_sub, blk, _ = q_ref.shape[1:]
    wide = 2 * blk

    def strict_lower(n):
        r = lax.broadcasted_iota(jnp.int32, (n, n), 0)
        c = lax.broadcasted_iota(jnp.int32, (n, n), 1)
        return jnp.where(r > c, 1.0, 0.0).astype(BF16)

    def scores(q, keys):
        return lax.dot_general(q, keys, (((1,), (1,)), ((), ())), preferred_element_type=F32)

    def log_terms(z):
        soft = jnp.log(1.0 + jnp.exp2(-jnp.abs(z))) * LOG2E
        lsz = jnp.minimum(z, 0.0) - soft
        return lsz, lsz - z

    tri = strict_lower(blk)
    row = lax.broadcasted_iota(jnp.int32, (blk, wide), 0)
    col = lax.broadcasted_iota(jnp.int32, (blk, wide), 1)
    causal = col < row + blk

    def first_tile(sub):
        i = pl.program_id(1) * n_sub + sub
        prev = jnp.maximum(i - 1, 0)
        lsz_all, l1m_all = [], []
        for hh in range(n_heads):
            keys = jnp.concatenate([k_ref[0, hh, prev], k_ref[0, hh, i]], axis=0)
            lsz, l1m = log_terms(scores(q_ref[0, hh, sub], keys))
            lsz_all.append(lsz)
            l1m_all.append(jnp.where(causal, l1m, 0.0))
        stack = lambda lo: jnp.concatenate([l1m[:, lo:lo + blk] for l1m in l1m_all],
                                           axis=0).astype(BF16)
        after_prev_all = jnp.dot(stack(0), tri, preferred_element_type=F32)
        after_diag_all = jnp.dot(stack(blk), tri, preferred_element_type=F32)
        carries, accs = [], []
        for hh in range(n_heads):
            l1m = l1m_all[hh]
            after_prev = after_prev_all[hh * blk:(hh + 1) * blk]
            after_diag = after_diag_all[hh * blk:(hh + 1) * blk]
            total_diag = after_diag[:, 0:1] + l1m[:, blk:blk + 1]
            after = jnp.concatenate([after_prev + total_diag, after_diag], axis=1)
            a = jnp.where(causal, jnp.exp2(lsz_all[hh] + after), 0.0)
            v_prev = v_ref[0, hh, prev]
            if sub == 0:
                v_prev = jnp.where(i > 0, v_prev, jnp.zeros((), BF16))
            vals = jnp.concatenate([v_prev, v_ref[0, hh, i]], axis=0)
            accs.append(jnp.dot(a.astype(BF16), vals, preferred_element_type=F32))
            carries.append(after_prev[:, 0:1] + total_diag + l1m[:, 0:1])
        o_ref[0, sub * blk:(sub + 1) * blk, :] = jnp.concatenate(accs, axis=-1).astype(BF16)
        return i, carries, accs

    def any_alive(carries):
        return jnp.max(functools.reduce(jnp.maximum, carries)) > DEAD_LOG2

    def earlier_blocks(sub, i, carries, accs):
        @pl.when(jnp.logical_and(i > 1, any_alive(carries)))
        def _():
            for hh in range(n_heads):
                carry_ref[hh] = carries[hh]
                acc_ref[hh] = accs[hh]

            def step(loop):
                it, _ = loop
                j = i - 2 - it
                new = []
                for hh in range(n_heads):
                    lsz, l1m = log_terms(scores(q_ref[0, hh, sub], k_ref[0, hh, j]))
                    after = jnp.dot(l1m.astype(BF16), tri, preferred_element_type=F32)
                    a = jnp.exp2(lsz + after + carry_ref[hh])
                    acc_ref[hh] += jnp.dot(a.astype(BF16), v_ref[0, hh, j],
                                           preferred_element_type=F32)
                    new.append(carry_ref[hh] + after[:, 0:1] + l1m[:, 0:1])
                    carry_ref[hh] = new[-1]
                return it + 1, any_alive(new)

            lax.while_loop(lambda loop: jnp.logical_and(loop[0] < i - 1, loop[1]),
                           step, (jnp.int32(0), True))
            o_ref[0, sub * blk:(sub + 1) * blk, :] = jnp.concatenate(
                [acc_ref[hh] for hh in range(n_heads)], axis=-1).astype(BF16)

    firsts = [first_tile(sub) for sub in range(n_sub)]
    for sub, (i, carries, accs) in enumerate(firsts):
        earlier_blocks(sub, i, carries, accs)


def _out_kernel(x_ref, osb_ref, gy_ref, gsb_ref, w_brs_ref, w_out_ref, g_post_mix_ref,
                g_pre_mlp_ref, w_up_ref, w_down_ref, g_post_mlp_ref, o_ref):
    def mixed(r):
        y_sb = jnp.dot(osb_ref[0, r], w_brs_ref[...], preferred_element_type=F32)
        merged = gy_ref[0, r].astype(F32) + gsb_ref[0, r].astype(F32) * y_sb
        return jnp.dot(merged.astype(BF16), w_out_ref[...], preferred_element_type=F32)

    def normed(r, mix):
        x1 = x_ref[0, r] + mix * _rms_scale(mix) * g_post_mix_ref[...]
        return x1, (x1 * _rms_scale(x1) * g_pre_mlp_ref[...]).astype(BF16)

    def mlp(h2):
        ff = None
        for c0 in range(0, w_up_ref.shape[1], FF_CHUNK):
            up = jnp.dot(h2, w_up_ref[:, c0:c0 + FF_CHUNK], preferred_element_type=F32)
            act = jnp.square(jnp.maximum(up, 0.0)).astype(BF16)
            part = jnp.dot(act, w_down_ref[c0:c0 + FF_CHUNK, :], preferred_element_type=F32)
            ff = part if ff is None else ff + part
        return ff

    half = x_ref.shape[1] // 2
    halves = (pl.ds(0, half), pl.ds(half, half))
    mixes = [mixed(r) for r in halves]
    normeds = [normed(r, mix) for r, mix in zip(halves, mixes)]
    for r, (x1, h2) in zip(halves, normeds):
        ff = mlp(h2)
        o_ref[0, r] = x1 + ff * _rms_scale(ff) * g_post_mlp_ref[...]


def _as_row(a):
    return a.reshape(1, -1).astype(F32)


def _tok_spec(width):
    return pl.BlockSpec((1, TOK_TILE, width), lambda b, s: (b, s, 0))


def _project(x, g_pre_mix, w_in, w_pool_mix, pool_scale, w_br_pool, w_gate, b_gate, n_heads):
    B, S, D = x.shape
    sub = TOK_TILE // ATT_BLK
    head_tile = pl.BlockSpec((1, n_heads, sub, ATT_BLK, HEAD_DIM), lambda b, s: (b, 0, s, 0, 0))
    head_shape = jax.ShapeDtypeStruct((B, n_heads, S // ATT_BLK, ATT_BLK, HEAD_DIM), BF16)
    return pl.pallas_call(
        functools.partial(_proj_kernel, n_heads=n_heads, d_model=D),
        grid=(B, S // TOK_TILE),
        in_specs=[
            _tok_spec(D),
            _const_spec((1, D)),
            _const_spec(w_in.shape),
            _const_spec(w_pool_mix.shape),
            _const_spec((1, POOL_WIDTH)),
            _const_spec(w_br_pool.shape),
            _const_spec(w_gate.shape),
            _const_spec((1, 2 * D)),
        ],
        out_specs=[head_tile, head_tile, head_tile, _tok_spec(D), _tok_spec(D)],
        out_shape=[head_shape, head_shape, head_shape,
                   jax.ShapeDtypeStruct((B, S, D), BF16),
                   jax.ShapeDtypeStruct((B, S, D), BF16)],
        scratch_shapes=[pltpu.VMEM((HALO + TOK_TILE, POOL_WIDTH), F32),
                        pltpu.VMEM((len(POOL_WINDOWS), HALO + TOK_TILE, POOL_GROUP), F32)],
        compiler_params=pltpu.CompilerParams(
            dimension_semantics=("arbitrary", "arbitrary"),
            vmem_limit_bytes=VMEM_LIMIT_BYTES),
        name="proj_pool_gates",
    )(x, _as_row(g_pre_mix), w_in.astype(BF16), w_pool_mix.astype(BF16), _as_row(pool_scale),
      w_br_pool.astype(BF16), w_gate.astype(BF16), _as_row(b_gate))


def _attention(q, k, v):
    B, n_heads, n_blk = q.shape[:3]
    assert n_blk % ATT_QBLKS == 0
    q_spec = pl.BlockSpec((1, n_heads, ATT_QBLKS, ATT_BLK, HEAD_DIM),
                          lambda b, i: (b, 0, i, 0, 0))
    seq_spec = pl.BlockSpec((1, n_heads, n_blk, ATT_BLK, HEAD_DIM), lambda b, i: (b, 0, 0, 0, 0))
    sb_width = n_heads * HEAD_DIM
    return pl.pallas_call(
        _attn_kernel,
        grid=(B, n_blk // ATT_QBLKS),
        in_specs=[q_spec, seq_spec, seq_spec],
        out_specs=pl.BlockSpec((1, ATT_QBLKS * ATT_BLK, sb_width), lambda b, i: (b, i, 0)),
        out_shape=jax.ShapeDtypeStruct((B, n_blk * ATT_BLK, sb_width), BF16),
        scratch_shapes=[pltpu.VMEM((n_heads, ATT_BLK, 1), F32),
                        pltpu.VMEM((n_heads, ATT_BLK, HEAD_DIM), F32)],
        compiler_params=pltpu.CompilerParams(
            dimension_semantics=("parallel", "parallel"),
            vmem_limit_bytes=VMEM_LIMIT_BYTES),
        name="stickbreak_attn",
    )(q, k, v)


def _merge_mlp(x, o_sb, gy, gsb, w_br_sb, w_out, g_post_mix, g_pre_mlp, w_up, w_down,
               g_post_mlp):
    B, S, D = x.shape
    return pl.pallas_call(
        _out_kernel,
        grid=(B, S // TOK_TILE),
        in_specs=[
            _tok_spec(D),
            _tok_spec(o_sb.shape[-1]),
            _tok_spec(D),
            _tok_spec(D),
            _const_spec(w_br_sb.shape),
            _const_spec(w_out.shape),
            _const_spec((1, D)),
            _const_spec((1, D)),
            _const_spec(w_up.shape),
            _const_spec(w_down.shape),
            _const_spec((1, D)),
        ],
        out_specs=_tok_spec(D),
        out_shape=jax.ShapeDtypeStruct((B, S, D), x.dtype),
        compiler_params=pltpu.CompilerParams(
            dimension_semantics=("parallel", "parallel"),
            vmem_limit_bytes=VMEM_LIMIT_BYTES),
        name="merge_mlp",
    )(x, o_sb, gy, gsb, w_br_sb.astype(BF16), w_out.astype(BF16), _as_row(g_post_mix),
      _as_row(g_pre_mlp), w_up.astype(BF16), w_down.astype(BF16), _as_row(g_post_mlp))


def kernel(x, g_pre_mix, w_in, w_pool_mix, pool_scale, w_br_pool, w_br_sb, w_gate, b_gate,
           w_out, g_post_mix, g_pre_mlp, w_up, w_down, g_post_mlp):
    B, S, D = x.shape
    sb_width = w_br_sb.shape[0]
    assert w_in.shape[1] == POOL_WIDTH + 3 * sb_width and w_br_pool.shape[0] == POOL_WIDTH
    assert S % TOK_TILE == 0 and TOK_TILE % ATT_BLK == 0 and w_gate.shape[1] == 2 * D
    assert w_up.shape[1] % FF_CHUNK == 0

    q, k, v, gy, gsb = _project(x, g_pre_mix, w_in, w_pool_mix, pool_scale, w_br_pool,
                                w_gate, b_gate, sb_width // HEAD_DIM)
    o_sb = _attention(q, k, v)
    return _merge_mlp(x, o_sb, gy, gsb, w_br_sb, w_out, g_post_mix, g_pre_mlp, w_up, w_down,
                      g_post_mlp)
```

```python
import functools

import jax
import jax.numpy as jnp
from jax import lax
from jax.experimental import pallas as pl
from jax.experimental.pallas import tpu as pltpu

RMS_EPS = 1e-6
POOL_WINDOWS = (2, 4, 8, 16)
POOL_GROUP = 128
POOL_WIDTH = POOL_GROUP * len(POOL_WINDOWS)
HEAD_DIM = 64
HALO = 32
TOK_TILE = 512
ATT_BLK = 256
ATT_QBLKS = 4
FF_CHUNK = 1024
DEAD_LOG2 = -151.0
LOG2E = 1.4426950408889634

VMEM_LIMIT_BYTES = 52 * 1024 * 1024

BF16 = jnp.bfloat16
F32 = jnp.float32


def _rms_scale(x):
    return lax.rsqrt(jnp.mean(x * x, axis=-1, keepdims=True) + RMS_EPS)


def _const_spec(shape):
    zeros = (0,) * len(shape)
    return pl.BlockSpec(shape, lambda *_: zeros, pipeline_mode=pl.Buffered(1))


def _proj_kernel(x_ref, g_pre_ref, w_in_ref, w_pm_ref, pscale_ref, w_brp_ref,
                 w_gate_ref, b_gate_ref,
                 q_ref, k_ref, v_ref, gy_ref, gsb_ref, uext_ref, lvl_ref, *, n_heads, d_model):
    s = pl.program_id(1)
    tm = x_ref.shape[1]
    sb_width = n_heads * HEAD_DIM

    @pl.when(jnp.logical_and(pl.program_id(0) == 0, s == 0))
    def _():
        uext_ref[tm:tm + HALO, :] = jnp.zeros((HALO, POOL_WIDTH), F32)

    hs = []
    for r in (pl.ds(0, tm // 2), pl.ds(tm // 2, tm // 2)):
        x = x_ref[0, r]
        hs.append((x * _rms_scale(x) * g_pre_ref[...]).astype(BF16))
    both = lambda w_ref: jnp.concatenate(
        [jnp.dot(h, w_ref[...], preferred_element_type=F32) for h in hs], axis=0)
    proj = both(w_in_ref)
    gates = jax.nn.sigmoid(both(w_gate_ref) + b_gate_ref[...])
    gsb_ref[0] = gates[:, d_model:].astype(BF16)

    q = proj[:, POOL_WIDTH:POOL_WIDTH + sb_width] * (HEAD_DIM ** -0.5 * LOG2E)
    k = proj[:, POOL_WIDTH + sb_width:POOL_WIDTH + 2 * sb_width]
    v = proj[:, POOL_WIDTH + 2 * sb_width:POOL_WIDTH + 3 * sb_width]
    for src, dst in ((q, q_ref), (k, k_ref), (v, v_ref)):
        for hh in range(n_heads):
            cols = src[:, hh * HEAD_DIM:(hh + 1) * HEAD_DIM].astype(BF16)
            for t in range(tm // ATT_BLK):
                dst[0, hh, t] = cols[t * ATT_BLK:(t + 1) * ATT_BLK]

    uext_ref[0:HALO, :] = jnp.where(s > 0, uext_ref[tm:tm + HALO, :], 0.0)

    u = proj[:, :POOL_WIDTH]
    uext_ref[HALO:HALO + tm, :] = u

    pos = s * tm + lax.broadcasted_iota(jnp.int32, (tm, POOL_GROUP), 0)
    rows = HALO + tm
    mixed = []
    for g, w in enumerate(POOL_WINDOWS):
        c0, c1 = g * POOL_GROUP, (g + 1) * POOL_GROUP
        levels = w.bit_length() - 1
        start = HALO - 8 * (levels - 1)
        win = uext_ref[start:rows, c0:c1] + uext_ref[start - 1:rows - 1, c0:c1]
        for lvl in range(2, levels + 1):
            back = 1 << (lvl - 1)
            lvl_ref[g, start:rows, :] = win
            start += 8
            win = win[8:] + lvl_ref[g, start - back:rows - back, :]
        count = jnp.minimum(pos + 1, w).astype(F32)
        pooled = win / count - u[:, c0:c1]
        mixed.append(jnp.dot(pooled.astype(BF16), w_pm_ref[g], preferred_element_type=F32))
    y = jnp.concatenate(mixed, axis=-1) * pscale_ref[...]
    y_pool = jnp.dot(y.astype(BF16), w_brp_ref[...], preferred_element_type=F32)

    gy_ref[0] = (gates[:, :d_model] * y_pool).astype(BF16)


def _attn_kernel(q_ref, k_ref, v_ref, o_ref, carry_ref, acc_ref):
    n_heads, n_sub, blk, _ = q_ref.shape[1:]
    wide = 2 * blk

    def strict_lower(n):
        r = lax.broadcasted_iota(jnp.int32, (n, n), 0)
        c = lax.broadcasted_iota(jnp.int32, (n, n), 1)
        return jnp.where(r > c, 1.0, 0.0).astype(BF16)

    def scores(q, keys):
        return lax.dot_general(q, keys, (((1,), (1,)), ((), ())), preferred_element_type=F32)

    def log_terms(z):
        soft = jnp.log(1.0 + jnp.exp2(-jnp.abs(z))) * LOG2E
        lsz = jnp.minimum(z, 0.0) - soft
        return lsz, lsz - z

    tri = strict_lower(blk)
    row = lax.broadcasted_iota(jnp.int32, (blk, wide), 0)
    col = lax.broadcasted_iota(jnp.int32, (blk, wide), 1)
    causal = col < row + blk

    def first_tile(sub):
        i = pl.program_id(1) * n_sub + sub
        prev = jnp.maximum(i - 1, 0)
        lsz_all, l1m_all = [], []
        for hh in range(n_heads):
            keys = jnp.concatenate([k_ref[0, hh, prev], k_ref[0, hh, i]], axis=0)
            lsz, l1m = log_terms(scores(q_ref[0, hh, sub], keys))
            lsz_all.append(lsz)
            l1m_all.append(jnp.where(causal, l1m, 0.0))
        stack = lambda lo: jnp.concatenate([l1m[:, lo:lo + blk] for l1m in l1m_all],
                                           axis=0).astype(BF16)
        after_prev_all = jnp.dot(stack(0), tri, preferred_element_type=F32)
        after_diag_all = jnp.dot(stack(blk), tri, preferred_element_type=F32)
        carries, accs = [], []
        for hh in range(n_heads):
            l1m = l1m_all[hh]
            after_prev = after_prev_all[hh * blk:(hh + 1) * blk]
            after_diag = after_diag_all[hh * blk:(hh + 1) * blk]
            total_diag = after_diag[:, 0:1] + l1m[:, blk:blk + 1]
            after = jnp.concatenate([after_prev + total_diag, after_diag], axis=1)
            a = jnp.where(causal, jnp.exp2(lsz_all[hh] + after), 0.0)
            v_prev = v_ref[0, hh, prev]
            if sub == 0:
                v_prev = jnp.where(i > 0, v_prev, jnp.zeros((), BF16))
            vals = jnp.concatenate([v_prev, v_ref[0, hh, i]], axis=0)
            accs.append(jnp.dot(a.astype(BF16), vals, preferred_element_type=F32))
            carries.append(after_prev[:, 0:1] + total_diag + l1m[:, 0:1])
        o_ref[0, sub * blk:(sub + 1) * blk, :] = jnp.concatenate(accs, axis=-1).astype(BF16)
        return i, carries, accs

    def any_alive(carries):
        return jnp.max(functools.reduce(jnp.maximum, carries)) > DEAD_LOG2

    def earlier_blocks(sub, i, carries, accs):
        @pl.when(jnp.logical_and(i > 1, any_alive(carries)))
        def _():
            for hh in range(n_heads):
                carry_ref[hh] = carries[hh]
                acc_ref[hh] = accs[hh]

            def step(loop):
                it, _ = loop
                j = i - 2 - it
                new = []
                for hh in range(n_heads):
                    lsz, l1m = log_terms(scores(q_ref[0, hh, sub], k_ref[0, hh, j]))
                    after = jnp.dot(l1m.astype(BF16), tri, preferred_element_type=F32)
                    a = jnp.exp2(lsz + after + carry_ref[hh])
                    acc_ref[hh] += jnp.dot(a.astype(BF16), v_ref[0, hh, j],
                                           preferred_element_type=F32)
                    new.append(carry_ref[hh] + after[:, 0:1] + l1m[:, 0:1])
                    carry_ref[hh] = new[-1]
                return it + 1, any_alive(new)

            lax.while_loop(lambda loop: jnp.logical_and(loop[0] < i - 1, loop[1]),
                           step, (jnp.int32(0), True))
            o_ref[0, sub * blk:(sub + 1) * blk, :] = jnp.concatenate(
                [acc_ref[hh] for hh in range(n_heads)], axis=-1).astype(BF16)

    firsts = [first_tile(sub) for sub in range(n_sub)]
    for sub, (i, carries, accs) in enumerate(firsts):
        earlier_blocks(sub, i, carries, accs)


def _out_kernel(x_ref, osb_ref, gy_ref, gsb_ref, w_brs_ref, w_out_ref, g_post_mix_ref,
                g_pre_mlp_ref, w_up_ref, w_down_ref, g_post_mlp_ref, o_ref):
    def mixed(r):
        y_sb = jnp.dot(osb_ref[0, r], w_brs_ref[...], preferred_element_type=F32)
        merged = gy_ref[0, r].astype(F32) + gsb_ref[0, r].astype(F32) * y_sb
        return jnp.dot(merged.astype(BF16), w_out_ref[...], preferred_element_type=F32)

    def normed(r, mix):
        x1 = x_ref[0, r] + mix * _rms_scale(mix) * g_post_mix_ref[...]
        return x1, (x1 * _rms_scale(x1) * g_pre_mlp_ref[...]).astype(BF16)

    def mlp(h2):
        ff = None
        for c0 in range(0, w_up_ref.shape[1], FF_CHUNK):
            up = jnp.dot(h2, w_up_ref[:, c0:c0 + FF_CHUNK], preferred_element_type=F32)
            act = jnp.square(jnp.maximum(up, 0.0)).astype(BF16)
            part = jnp.dot(act, w_down_ref[c0:c0 + FF_CHUNK, :], preferred_element_type=F32)
            ff = part if ff is None else ff + part
        return ff

    half = x_ref.shape[1] // 2
    halves = (pl.ds(0, half), pl.ds(half, half))
    mixes = [mixed(r) for r in halves]
    normeds = [normed(r, mix) for r, mix in zip(halves, mixes)]
    for r, (x1, h2) in zip(halves, normeds):
        ff = mlp(h2)
        o_ref[0, r] = x1 + ff * _rms_scale(ff) * g_post_mlp_ref[...]


def _as_row(a):
    return a.reshape(1, -1).astype(F32)


def _tok_spec(width):
    return pl.BlockSpec((1, TOK_TILE, width), lambda b, s: (b, s, 0))


def _project(x, g_pre_mix, w_in, w_pool_mix, pool_scale, w_br_pool, w_gate, b_gate, n_heads):
    B, S, D = x.shape
    sub = TOK_TILE // ATT_BLK
    head_tile = pl.BlockSpec((1, n_heads, sub, ATT_BLK, HEAD_DIM), lambda b, s: (b, 0, s, 0, 0))
    head_shape = jax.ShapeDtypeStruct((B, n_heads, S // ATT_BLK, ATT_BLK, HEAD_DIM), BF16)
    return pl.pallas_call(
        functools.partial(_proj_kernel, n_heads=n_heads, d_model=D),
        grid=(B, S // TOK_TILE),
        in_specs=[
            _tok_spec(D),
            _const_spec((1, D)),
            _const_spec(w_in.shape),
            _const_spec(w_pool_mix.shape),
            _const_spec((1, POOL_WIDTH)),
            _const_spec(w_br_pool.shape),
            _const_spec(w_gate.shape),
            _const_spec((1, 2 * D)),
        ],
        out_specs=[head_tile, head_tile, head_tile, _tok_spec(D), _tok_spec(D)],
        out_shape=[head_shape, head_shape, head_shape,
                   jax.ShapeDtypeStruct((B, S, D), BF16),
                   jax.ShapeDtypeStruct((B, S, D), BF16)],
        scratch_shapes=[pltpu.VMEM((HALO + TOK_TILE, POOL_WIDTH), F32),
                        pltpu.VMEM((len(POOL_WINDOWS), HALO + TOK_TILE, POOL_GROUP), F32)],
        compiler_params=pltpu.CompilerParams(
            dimension_semantics=("arbitrary", "arbitrary"),
            vmem_limit_bytes=VMEM_LIMIT_BYTES),
        name="proj_pool_gates",
    )(x, _as_row(g_pre_mix), w_in.astype(BF16), w_pool_mix.astype(BF16), _as_row(pool_scale),
      w_br_pool.astype(BF16), w_gate.astype(BF16), _as_row(b_gate))


def _attention(q, k, v):
    B, n_heads, n_blk = q.shape[:3]
    assert n_blk % ATT_QBLKS == 0
    q_spec = pl.BlockSpec((1, n_heads, ATT_QBLKS, ATT_BLK, HEAD_DIM),
                          lambda b, i: (b, 0, i, 0, 0))
    seq_spec = pl.BlockSpec((1, n_heads, n_blk, ATT_BLK, HEAD_DIM), lambda b, i: (b, 0, 0, 0, 0))
    sb_width = n_heads * HEAD_DIM
    return pl.pallas_call(
        _attn_kernel,
        grid=(B, n_blk // ATT_QBLKS),
        in_specs=[q_spec, seq_spec, seq_spec],
        out_specs=pl.BlockSpec((1, ATT_QBLKS * ATT_BLK, sb_width), lambda b, i: (b, i, 0)),
        out_shape=jax.ShapeDtypeStruct((B, n_blk * ATT_BLK, sb_width), BF16),
        scratch_shapes=[pltpu.VMEM((n_heads, ATT_BLK, 1), F32),
                        pltpu.VMEM((n_heads, ATT_BLK, HEAD_DIM), F32)],
        compiler_params=pltpu.CompilerParams(
            dimension_semantics=("parallel", "parallel"),
            vmem_limit_bytes=VMEM_LIMIT_BYTES),
        name="stickbreak_attn",
    )(q, k, v)


def _merge_mlp(x, o_sb, gy, gsb, w_br_sb, w_out, g_post_mix, g_pre_mlp, w_up, w_down,
               g_post_mlp):
    B, S, D = x.shape
    return pl.pallas_call(
        _out_kernel,
        grid=(B, S // TOK_TILE),
        in_specs=[
            _tok_spec(D),
            _tok_spec(o_sb.shape[-1]),
            _tok_spec(D),
            _tok_spec(D),
            _const_spec(w_br_sb.shape),
            _const_spec(w_out.shape),
            _const_spec((1, D)),
            _const_spec((1, D)),
            _const_spec(w_up.shape),
            _const_spec(w_down.shape),
            _const_spec((1, D)),
        ],
        out_specs=_tok_spec(D),
        out_shape=jax.ShapeDtypeStruct((B, S, D), x.dtype),
        compiler_params=pltpu.CompilerParams(
            dimension_semantics=("parallel", "parallel"),
            vmem_limit_bytes=VMEM_LIMIT_BYTES),
        name="merge_mlp",
    )(x, o_sb, gy, gsb, w_br_sb.astype(BF16), w_out.astype(BF16), _as_row(g_post_mix),
      _as_row(g_pre_mlp), w_up.astype(BF16), w_down.astype(BF16), _as_row(g_post_mlp))


def kernel(x, g_pre_mix, w_in, w_pool_mix, pool_scale, w_br_pool, w_br_sb, w_gate, b_gate,
           w_out, g_post_mix, g_pre_mlp, w_up, w_down, g_post_mlp):
    B, S, D = x.shape
    sb_width = w_br_sb.shape[0]
    assert w_in.shape[1] == POOL_WIDTH + 3 * sb_width and w_br_pool.shape[0] == POOL_WIDTH
    assert S % TOK_TILE == 0 and TOK_TILE % ATT_BLK == 0 and w_gate.shape[1] == 2 * D
    assert w_up.shape[1] % FF_CHUNK == 0

    q, k, v, gy, gsb = _project(x, g_pre_mix, w_in, w_pool_mix, pool_scale, w_br_pool,
                                w_gate, b_gate, sb_width // HEAD_DIM)
    o_sb = _attention(q, k, v)
    return _merge_mlp(x, o_sb, gy, gsb, w_br_sb, w_out, g_post_mix, g_pre_mlp, w_up, w_down,
                      g_post_mlp)
```

```python
import functools

import jax
import jax.numpy as jnp
from jax import lax
from jax.experimental import pallas as pl
from jax.experimental.pallas import tpu as pltpu

RMS_EPS = 1e-6
POOL_WINDOWS = (2, 4, 8, 16)
POOL_GROUP = 128
POOL_WIDTH = POOL_GROUP * len(POOL_WINDOWS)
HEAD_DIM = 64
HALO = 32
TOK_TILE = 512
ATT_BLK = 256
ATT_QBLKS = 2
FF_CHUNK = 1024
DEAD_LOG2 = -151.0
LOG2E = 1.4426950408889634

VMEM_LIMIT_BYTES = 52 * 1024 * 1024

BF16 = jnp.bfloat16
F32 = jnp.float32


def _rms_scale(x):
    return lax.rsqrt(jnp.mean(x * x, axis=-1, keepdims=True) + RMS_EPS)


def _const_spec(shape):
    zeros = (0,) * len(shape)
    return pl.BlockSpec(shape, lambda *_: zeros, pipeline_mode=pl.Buffered(1))


def _proj_kernel(x_ref, g_pre_ref, w_in_ref, w_pm_ref, pscale_ref, w_brp_ref,
                 w_gate_ref, b_gate_ref,
                 q_ref, k_ref, v_ref, gy_ref, gsb_ref, uext_ref, lvl_ref, *, n_heads, d_model):
    s = pl.program_id(1)
    tm = x_ref.shape[1]
    sb_width = n_heads * HEAD_DIM

    @pl.when(jnp.logical_and(pl.program_id(0) == 0, s == 0))
    def _():
        uext_ref[tm:tm + HALO, :] = jnp.zeros((HALO, POOL_WIDTH), F32)

    hs = []
    for r in (pl.ds(0, tm // 2), pl.ds(tm // 2, tm // 2)):
        x = x_ref[0, r]
        hs.append((x * _rms_scale(x) * g_pre_ref[...]).astype(BF16))
    both = lambda w_ref: jnp.concatenate(
        [jnp.dot(h, w_ref[...], preferred_element_type=F32) for h in hs], axis=0)
    proj = both(w_in_ref)
    gates = jax.nn.sigmoid(both(w_gate_ref) + b_gate_ref[...])
    gsb_ref[0] = gates[:, d_model:].astype(BF16)

    q = proj[:, POOL_WIDTH:POOL_WIDTH + sb_width] * (HEAD_DIM ** -0.5 * LOG2E)
    k = proj[:, POOL_WIDTH + sb_width:POOL_WIDTH + 2 * sb_width]
    v = proj[:, POOL_WIDTH + 2 * sb_width:POOL_WIDTH + 3 * sb_width]
    for src, dst in ((q, q_ref), (k, k_ref), (v, v_ref)):
        for hh in range(n_heads):
            cols = src[:, hh * HEAD_DIM:(hh + 1) * HEAD_DIM].astype(BF16)
            for t in range(tm // ATT_BLK):
                dst[0, hh, t] = cols[t * ATT_BLK:(t + 1) * ATT_BLK]

    uext_ref[0:HALO, :] = jnp.where(s > 0, uext_ref[tm:tm + HALO, :], 0.0)

    u = proj[:, :POOL_WIDTH]
    uext_ref[HALO:HALO + tm, :] = u

    pos = s * tm + lax.broadcasted_iota(jnp.int32, (tm, POOL_GROUP), 0)
    rows = HALO + tm
    mixed = []
    for g, w in enumerate(POOL_WINDOWS):
        c0, c1 = g * POOL_GROUP, (g + 1) * POOL_GROUP
        levels = w.bit_length() - 1
        start = HALO - 8 * (levels - 1)
        win = uext_ref[start:rows, c0:c1] + uext_ref[start - 1:rows - 1, c0:c1]
        for lvl in range(2, levels + 1):
            back = 1 << (lvl - 1)
            lvl_ref[g, start:rows, :] = win
            start += 8
            win = win[8:] + lvl_ref[g, start - back:rows - back, :]
        count = jnp.minimum(pos + 1, w).astype(F32)
        pooled = win / count - u[:, c0:c1]
        mixed.append(jnp.dot(pooled.astype(BF16), w_pm_ref[g], preferred_element_type=F32))
    y = jnp.concatenate(mixed, axis=-1) * pscale_ref[...]
    y_pool = jnp.dot(y.astype(BF16), w_brp_ref[...], preferred_element_type=F32)

    gy_ref[0] = (gates[:, :d_model] * y_pool).astype(BF16)


def _attn_kernel(q_ref, k_ref, v_ref, o_ref, carry_ref, acc_ref):
    n_heads, n_sub, blk, _ = q_ref.shape[1:]
    wide = 2 * blk

    def strict_lower(n):
        r = lax.broadcasted_iota(jnp.int32, (n, n), 0)
        c = lax.broadcasted_iota(jnp.int32, (n, n), 1)
        return jnp.where(r > c, 1.0, 0.0).astype(BF16)

    def scores(q, keys):
        return lax.dot_general(q, keys, (((1,), (1,)), ((), ())), preferred_element_type=F32)

    def log_terms(z):
        soft = jnp.log(1.0 + jnp.exp2(-jnp.abs(z))) * LOG2E
        lsz = jnp.minimum(z, 0.0) - soft
        return lsz, lsz - z

    tri = strict_lower(blk)
    row = lax.broadcasted_iota(jnp.int32, (blk, wide), 0)
    col = lax.broadcasted_iota(jnp.int32, (blk, wide), 1)
    causal = col < row + blk

    def first_tile(sub):
        i = pl.program_id(1) * n_sub + sub
        prev = jnp.maximum(i - 1, 0)
        lsz_all, l1m_all = [], []
        for hh in range(n_heads):
            keys = jnp.concatenate([k_ref[0, hh, prev], k_ref[0, hh, i]], axis=0)
            lsz, l1m = log_terms(scores(q_ref[0, hh, sub], keys))
            lsz_all.append(lsz)
            l1m_all.append(jnp.where(causal, l1m, 0.0))
        stack = lambda lo: jnp.concatenate([l1m[:, lo:lo + blk] for l1m in l1m_all],
                                           axis=0).astype(BF16)
        after_prev_all = jnp.dot(stack(0), tri, preferred_element_type=F32)
        after_diag_all = jnp.dot(stack(blk), tri, preferred_element_type=F32)
        carries, accs = [], []
        for hh in range(n_heads):
            l1m = l1m_all[hh]
            after_prev = after_prev_all[hh * blk:(hh + 1) * blk]
            after_diag = after_diag_all[hh * blk:(hh + 1) * blk]
            total_diag = after_diag[:, 0:1] + l1m[:, blk:blk + 1]
            after = jnp.concatenate([after_prev + total_diag, after_diag], axis=1)
            a = jnp.where(causal, jnp.exp2(lsz_all[hh] + after), 0.0)
            v_prev = v_ref[0, hh, prev]
            if sub == 0:
                v_prev = jnp.where(i > 0, v_prev, jnp.zeros((), BF16))
            vals = jnp.concatenate([v_prev, v_ref[0, hh, i]], axis=0)
            accs.append(jnp.dot(a.astype(BF16), vals, preferred_element_type=F32))
            carries.append(after_prev[:, 0:1] + total_diag + l1m[:, 0:1])
        o_ref[0, sub * blk:(sub + 1) * blk, :] = jnp.concatenate(accs, axis=-1).astype(BF16)
        return i, carries, accs

    def any_alive(carries):
        return jnp.max(functools.reduce(jnp.maximum, carries)) > DEAD_LOG2

    def earlier_blocks(sub, i, carries, accs):
        @pl.when(jnp.logical_and(i > 1, any_alive(carries)))
        def _():
            for hh in range(n_heads):
                carry_ref[hh] = carries[hh]
                acc_ref[hh] = accs[hh]

            def step(loop):
                it, _ = loop
                j = i - 2 - it
                new = []
                for hh in range(n_heads):
                    lsz, l1m = log_terms(scores(q_ref[0, hh, sub], k_ref[0, hh, j]))
                    after = jnp.dot(l1m.astype(BF16), tri, preferred_element_type=F32)
                    a = jnp.exp2(lsz + after + carry_ref[hh])
                    acc_ref[hh] += jnp.dot(a.astype(BF16), v_ref[0, hh, j],
                                           preferred_element_type=F32)
                    new.append(carry_ref[hh] + after[:, 0:1] + l1m[:, 0:1])
                    carry_ref[hh] = new[-1]
                return it + 1, any_alive(new)

            lax.while_loop(lambda loop: jnp.logical_and(loop[0] < i - 1, loop[1]),
                           step, (jnp.int32(0), True))
            o_ref[0, sub * blk:(sub + 1) * blk, :] = jnp.concatenate(
                [acc_ref[hh] for hh in range(n_heads)], axis=-1).astype(BF16)

    firsts = [first_tile(sub) for sub in range(n_sub)]
    for sub, (i, carries, accs) in enumerate(firsts):
        earlier_blocks(sub, i, carries, accs)


def _out_kernel(x_ref, osb_ref, gy_ref, gsb_ref, w_brs_ref, w_out_ref, g_post_mix_ref,
                g_pre_mlp_ref, w_up_ref, w_down_ref, g_post_mlp_ref, o_ref):
    def mixed(r):
        y_sb = jnp.dot(osb_ref[0, r], w_brs_ref[...], preferred_element_type=F32)
        merged = gy_ref[0, r].astype(F32) + gsb_ref[0, r].astype(F32) * y_sb
        return jnp.dot(merged.astype(BF16), w_out_ref[...], preferred_element_type=F32)

    def normed(r, mix):
        x1 = x_ref[0, r] + mix * _rms_scale(mix) * g_post_mix_ref[...]
        return x1, (x1 * _rms_scale(x1) * g_pre_mlp_ref[...]).astype(BF16)

    def mlp(h2):
        ff = None
        for c0 in range(0, w_up_ref.shape[1], FF_CHUNK):
            up = jnp.dot(h2, w_up_ref[:, c0:c0 + FF_CHUNK], preferred_element_type=F32)
            act = jnp.square(jnp.maximum(up, 0.0)).astype(BF16)
            part = jnp.dot(act, w_down_ref[c0:c0 + FF_CHUNK, :], preferred_element_type=F32)
            ff = part if ff is None else ff + part
        return ff

    half = x_ref.shape[1] // 2
    halves = (pl.ds(0, half), pl.ds(half, half))
    mixes = [mixed(r) for r in halves]
    normeds = [normed(r, mix) for r, mix in zip(halves, mixes)]
    for r, (x1, h2) in zip(halves, normeds):
        ff = mlp(h2)
        o_ref[0, r] = x1 + ff * _rms_scale(ff) * g_post_mlp_ref[...]


def _as_row(a):
    return a.reshape(1, -1).astype(F32)


def _tok_spec(width):
    return pl.BlockSpec((1, TOK_TILE, width), lambda b, s: (b, s, 0))


def _project(x, g_pre_mix, w_in, w_pool_mix, pool_scale, w_br_pool, w_gate, b_gate, n_heads):
    B, S, D = x.shape
    sub = TOK_TILE // ATT_BLK
    head_tile = pl.BlockSpec((1, n_heads, sub, ATT_BLK, HEAD_DIM), lambda b, s: (b, 0, s, 0, 0))
    head_shape = jax.ShapeDtypeStruct((B, n_heads, S // ATT_BLK, ATT_BLK, HEAD_DIM), BF16)
    return pl.pallas_call(
        functools.partial(_proj_kernel, n_heads=n_heads, d_model=D),
        grid=(B, S // TOK_TILE),
        in_specs=[
            _tok_spec(D),
            _const_spec((1, D)),
            _const_spec(w_in.shape),
            _const_spec(w_pool_mix.shape),
            _const_spec((1, POOL_WIDTH)),
            _const_spec(w_br_pool.shape),
            _const_spec(w_gate.shape),
            _const_spec((1, 2 * D)),
        ],
        out_specs=[head_tile, head_tile, head_tile, _tok_spec(D), _tok_spec(D)],
        out_shape=[head_shape, head_shape, head_shape,
                   jax.ShapeDtypeStruct((B, S, D), BF16),
                   jax.ShapeDtypeStruct((B, S, D), BF16)],
        scratch_shapes=[pltpu.VMEM((HALO + TOK_TILE, POOL_WIDTH), F32),
                        pltpu.VMEM((len(POOL_WINDOWS), HALO + TOK_TILE, POOL_GROUP), F32)],
        compiler_params=pltpu.CompilerParams(
            dimension_semantics=("arbitrary", "arbitrary"),
            vmem_limit_bytes=VMEM_LIMIT_BYTES),
        name="proj_pool_gates",
    )(x, _as_row(g_pre_mix), w_in.astype(BF16), w_pool_mix.astype(BF16), _as_row(pool_scale),
      w_br_pool.astype(BF16), w_gate.astype(BF16), _as_row(b_gate))


def _attention(q, k, v):
    B, n_heads, n_blk = q.shape[:3]
    assert n_blk % ATT_QBLKS == 0
    q_spec = pl.BlockSpec((1, n_heads, ATT_QBLKS, ATT_BLK, HEAD_DIM),
                          lambda b, i: (b, 0, i, 0, 0))
    seq_spec = pl.BlockSpec((1, n_heads, n_blk, ATT_BLK, HEAD_DIM), lambda b, i: (b, 0, 0, 0, 0))
    sb_width = n_heads * HEAD_DIM
    return pl.pallas_call(
        _attn_kernel,
        grid=(B, n_blk // ATT_QBLKS),
        in_specs=[q_spec, seq_spec, seq_spec],
        out_specs=pl.BlockSpec((1, ATT_QBLKS * ATT_BLK, sb_width), lambda b, i: (b, i, 0)),
        out_shape=jax.ShapeDtypeStruct((B, n_blk * ATT_BLK, sb_width), BF16),
        scratch_shapes=[pltpu.VMEM((n_heads, ATT_BLK, 1), F32),
                        pltpu.VMEM((n_heads, ATT_BLK, HEAD_DIM), F32)],
        compiler_params=pltpu.CompilerParams(
            dimension_semantics=("parallel", "parallel"),
            vmem_limit_bytes=VMEM_LIMIT_BYTES),
        name="stickbreak_attn",
    )(q, k, v)


def _merge_mlp(x, o_sb, gy, gsb, w_br_sb, w_out, g_post_mix, g_pre_mlp, w_up, w_down,
               g_post_mlp):
    B, S, D = x.shape
    return pl.pallas_call(
        _out_kernel,
        grid=(B, S // TOK_TILE),
        in_specs=[
            _tok_spec(D),
            _tok_spec(o_sb.shape[-1]),
            _tok_spec(D),
            _tok_spec(D),
            _const_spec(w_br_sb.shape),
            _const_spec(w_out.shape),
            _const_spec((1, D)),
            _const_spec((1, D)),
            _const_spec(w_up.shape),
            _const_spec(w_down.shape),
            _const_spec((1, D)),
        ],
        out_specs=_tok_spec(D),
        out_shape=jax.ShapeDtypeStruct((B, S, D), x.dtype),
        compiler_params=pltpu.CompilerParams(
            dimension_semantics=("parallel", "parallel"),
            vmem_limit_bytes=VMEM_LIMIT_BYTES),
        name="merge_mlp",
    )(x, o_sb, gy, gsb, w_br_sb.astype(BF16), w_out.astype(BF16), _as_row(g_post_mix),
      _as_row(g_pre_mlp), w_up.astype(BF16), w_down.astype(BF16), _as_row(g_post_mlp))


def kernel(x, g_pre_mix, w_in, w_pool_mix, pool_scale, w_br_pool, w_br_sb, w_gate, b_gate,
           w_out, g_post_mix, g_pre_mlp, w_up, w_down, g_post_mlp):
    B, S, D = x.shape
    sb_width = w_br_sb.shape[0]
    assert w_in.shape[1] == POOL_WIDTH + 3 * sb_width and w_br_pool.shape[0] == POOL_WIDTH
    assert S % TOK_TILE == 0 and TOK_TILE % ATT_BLK == 0 and w_gate.shape[1] == 2 * D
    assert w_up.shape[1] % FF_CHUNK == 0

    q, k, v, gy, gsb = _project(x, g_pre_mix, w_in, w_pool_mix, pool_scale, w_br_pool,
                                w_gate, b_gate, sb_width // HEAD_DIM)
    o_sb = _attention(q, k, v)
    return _merge_mlp(x, o_sb, gy, gsb, w_br_sb, w_out, g_post_mix, g_pre_mlp, w_up, w_down,
                      g_post_mlp)
```

```python
import functools

import jax
import jax.numpy as jnp
from jax import lax
from jax.experimental import pallas as pl
from jax.experimental.pallas import tpu as pltpu

RMS_EPS = 1e-6
POOL_WINDOWS = (2, 4, 8, 16)
POOL_GROUP = 128
POOL_WIDTH = POOL_GROUP * len(POOL_WINDOWS)
HEAD_DIM = 64
HALO = 32
TOK_TILE = 1024
ATT_BLK = 256
ATT_QBLKS = 2
FF_CHUNK = 1024
DEAD_LOG2 = -151.0
LOG2E = 1.4426950408889634

VMEM_LIMIT_BYTES = 58 * 1024 * 1024

BF16 = jnp.bfloat16
F32 = jnp.float32


def _rms_scale(x):
    return lax.rsqrt(jnp.mean(x * x, axis=-1, keepdims=True) + RMS_EPS)


def _const_spec(shape):
    zeros = (0,) * len(shape)
    return pl.BlockSpec(shape, lambda *_: zeros, pipeline_mode=pl.Buffered(1))


def _proj_kernel(x_ref, g_pre_ref, w_in_ref, w_pm_ref, pscale_ref, w_brp_ref,
                 w_gate_ref, b_gate_ref,
                 q_ref, k_ref, v_ref, gy_ref, gsb_ref, uext_ref, lvl_ref, *, n_heads, d_model):
    s = pl.program_id(1)
    tm = x_ref.shape[1]
    sb_width = n_heads * HEAD_DIM

    @pl.when(jnp.logical_and(pl.program_id(0) == 0, s == 0))
    def _():
        uext_ref[tm:tm + HALO, :] = jnp.zeros((HALO, POOL_WIDTH), F32)

    hs = []
    for r in (pl.ds(0, tm // 2), pl.ds(tm // 2, tm // 2)):
        x = x_ref[0, r]
        hs.append((x * _rms_scale(x) * g_pre_ref[...]).astype(BF16))
    both = lambda w_ref: jnp.concatenate(
        [jnp.dot(h, w_ref[...], preferred_element_type=F32) for h in hs], axis=0)
    proj = both(w_in_ref)
    gates = jax.nn.sigmoid(both(w_gate_ref) + b_gate_ref[...])
    gsb_ref[0] = gates[:, d_model:].astype(BF16)

    q = proj[:, POOL_WIDTH:POOL_WIDTH + sb_width] * (HEAD_DIM ** -0.5 * LOG2E)
    k = proj[:, POOL_WIDTH + sb_width:POOL_WIDTH + 2 * sb_width]
    v = proj[:, POOL_WIDTH + 2 * sb_width:POOL_WIDTH + 3 * sb_width]
    for src, dst in ((q, q_ref), (k, k_ref), (v, v_ref)):
        for hh in range(n_heads):
            cols = src[:, hh * HEAD_DIM:(hh + 1) * HEAD_DIM].astype(BF16)
            for t in range(tm // ATT_BLK):
                dst[0, hh, t] = cols[t * ATT_BLK:(t + 1) * ATT_BLK]

    uext_ref[0:HALO, :] = jnp.where(s > 0, uext_ref[tm:tm + HALO, :], 0.0)

    u = proj[:, :POOL_WIDTH]
    uext_ref[HALO:HALO + tm, :] = u

    pos = s * tm + lax.broadcasted_iota(jnp.int32, (tm, POOL_GROUP), 0)
    rows = HALO + tm
    mixed = []
    for g, w in enumerate(POOL_WINDOWS):
        c0, c1 = g * POOL_GROUP, (g + 1) * POOL_GROUP
        levels = w.bit_length() - 1
        start = HALO - 8 * (levels - 1)
        win = uext_ref[start:rows, c0:c1] + uext_ref[start - 1:rows - 1, c0:c1]
        for lvl in range(2, levels + 1):
            back = 1 << (lvl - 1)
            lvl_ref[g, start:rows, :] = win
            start += 8
            win = win[8:] + lvl_ref[g, start - back:rows - back, :]
        count = jnp.minimum(pos + 1, w).astype(F32)
        pooled = win / count - u[:, c0:c1]
        mixed.append(jnp.dot(pooled.astype(BF16), w_pm_ref[g], preferred_element_type=F32))
    y = jnp.concatenate(mixed, axis=-1) * pscale_ref[...]
    y_pool = jnp.dot(y.astype(BF16), w_brp_ref[...], preferred_element_type=F32)

    gy_ref[0] = (gates[:, :d_model] * y_pool).astype(BF16)


def _attn_kernel(q_ref, k_ref, v_ref, o_ref, carry_ref, acc_ref):
    n_heads, n_sub, blk, _ = q_ref.shape[1:]
    wide = 2 * blk

    def strict_lower(n):
        r = lax.broadcasted_iota(jnp.int32, (n, n), 0)
        c = lax.broadcasted_iota(jnp.int32, (n, n), 1)
        return jnp.where(r > c, 1.0, 0.0).astype(BF16)

    def scores(q, keys):
        return lax.dot_general(q, keys, (((1,), (1,)), ((), ())), preferred_element_type=F32)

    def log_terms(z):
        soft = jnp.log(1.0 + jnp.exp2(-jnp.abs(z))) * LOG2E
        lsz = jnp.minimum(z, 0.0) - soft
        return lsz, lsz - z

    tri = strict_lower(blk)
    row = lax.broadcasted_iota(jnp.int32, (blk, wide), 0)
    col = lax.broadcasted_iota(jnp.int32, (blk, wide), 1)
    causal = col < row + blk

    def first_tile(sub):
        i = pl.program_id(1) * n_sub + sub
        prev = jnp.maximum(i - 1, 0)
        lsz_all, l1m_all = [], []
        for hh in range(n_heads):
            keys = jnp.concatenate([k_ref[0, hh, prev], k_ref[0, hh, i]], axis=0)
            lsz, l1m = log_terms(scores(q_ref[0, hh, sub], keys))
            lsz_all.append(lsz)
            l1m_all.append(jnp.where(causal, l1m, 0.0))
        stack = lambda lo: jnp.concatenate([l1m[:, lo:lo + blk] for l1m in l1m_all],
                                           axis=0).astype(BF16)
        after_prev_all = jnp.dot(stack(0), tri, preferred_element_type=F32)
        after_diag_all = jnp.dot(stack(blk), tri, preferred_element_type=F32)
        carries, accs = [], []
        for hh in range(n_heads):
            l1m = l1m_all[hh]
            after_prev = after_prev_all[hh * blk:(hh + 1) * blk]
            after_diag = after_diag_all[hh * blk:(hh + 1) * blk]
            total_diag = after_diag[:, 0:1] + l1m[:, blk:blk + 1]
            after = jnp.concatenate([after_prev + total_diag, after_diag], axis=1)
            a = jnp.where(causal, jnp.exp2(lsz_all[hh] + after), 0.0)
            v_prev = v_ref[0, hh, prev]
            if sub == 0:
                v_prev = jnp.where(i > 0, v_prev, jnp.zeros((), BF16))
            vals = jnp.concatenate([v_prev, v_ref[0, hh, i]], axis=0)
            accs.append(jnp.dot(a.astype(BF16), vals, preferred_element_type=F32))
            carries.append(after_prev[:, 0:1] + total_diag + l1m[:, 0:1])
        o_ref[0, sub * blk:(sub + 1) * blk, :] = jnp.concatenate(accs, axis=-1).astype(BF16)
        return i, carries, accs

    def any_alive(carries):
        return jnp.max(functools.reduce(jnp.maximum, carries)) > DEAD_LOG2

    def earlier_blocks(sub, i, carries, accs):
        @pl.when(jnp.logical_and(i > 1, any_alive(carries)))
        def _():
            for hh in range(n_heads):
                carry_ref[hh] = carries[hh]
                acc_ref[hh] = accs[hh]

            def step(loop):
                it, _ = loop
                j = i - 2 - it
                new = []
                for hh in range(n_heads):
                    lsz, l1m = log_terms(scores(q_ref[0, hh, sub], k_ref[0, hh, j]))
                    after = jnp.dot(l1m.astype(BF16), tri, preferred_element_type=F32)
                    a = jnp.exp2(lsz + after + carry_ref[hh])
                    acc_ref[hh] += jnp.dot(a.astype(BF16), v_ref[0, hh, j],
                                           preferred_element_type=F32)
                    new.append(carry_ref[hh] + after[:, 0:1] + l1m[:, 0:1])
                    carry_ref[hh] = new[-1]
                return it + 1, any_alive(new)

            lax.while_loop(lambda loop: jnp.logical_and(loop[0] < i - 1, loop[1]),
                           step, (jnp.int32(0), True))
            o_ref[0, sub * blk:(sub + 1) * blk, :] = jnp.concatenate(
                [acc_ref[hh] for hh in range(n_heads)], axis=-1).astype(BF16)

    firsts = [first_tile(sub) for sub in range(n_sub)]
    for sub, (i, carries, accs) in enumerate(firsts):
        earlier_blocks(sub, i, carries, accs)


def _out_kernel(x_ref, osb_ref, gy_ref, gsb_ref, w_brs_ref, w_out_ref, g_post_mix_ref,
                g_pre_mlp_ref, w_up_ref, w_down_ref, g_post_mlp_ref, o_ref):
    def mixed(r):
        y_sb = jnp.dot(osb_ref[0, r], w_brs_ref[...], preferred_element_type=F32)
        merged = gy_ref[0, r].astype(F32) + gsb_ref[0, r].astype(F32) * y_sb
        return jnp.dot(merged.astype(BF16), w_out_ref[...], preferred_element_type=F32)

    def normed(r, mix):
        x1 = x_ref[0, r] + mix * _rms_scale(mix) * g_post_mix_ref[...]
        return x1, (x1 * _rms_scale(x1) * g_pre_mlp_ref[...]).astype(BF16)

    def mlp(h2):
        ff = None
        for c0 in range(0, w_up_ref.shape[1], FF_CHUNK):
            up = jnp.dot(h2, w_up_ref[:, c0:c0 + FF_CHUNK], preferred_element_type=F32)
            act = jnp.square(jnp.maximum(up, 0.0)).astype(BF16)
            part = jnp.dot(act, w_down_ref[c0:c0 + FF_CHUNK, :], preferred_element_type=F32)
            ff = part if ff is None else ff + part
        return ff

    half = x_ref.shape[1] // 2
    halves = (pl.ds(0, half), pl.ds(half, half))
    mixes = [mixed(r) for r in halves]
    normeds = [normed(r, mix) for r, mix in zip(halves, mixes)]
    for r, (x1, h2) in zip(halves, normeds):
        ff = mlp(h2)
        o_ref[0, r] = x1 + ff * _rms_scale(ff) * g_post_mlp_ref[...]


def _as_row(a):
    return a.reshape(1, -1).astype(F32)


def _tok_spec(width):
    return pl.BlockSpec((1, TOK_TILE, width), lambda b, s: (b, s, 0))


def _project(x, g_pre_mix, w_in, w_pool_mix, pool_scale, w_br_pool, w_gate, b_gate, n_heads):
    B, S, D = x.shape
    sub = TOK_TILE // ATT_BLK
    head_tile = pl.BlockSpec((1, n_heads, sub, ATT_BLK, HEAD_DIM), lambda b, s: (b, 0, s, 0, 0))
    head_shape = jax.ShapeDtypeStruct((B, n_heads, S // ATT_BLK, ATT_BLK, HEAD_DIM), BF16)
    return pl.pallas_call(
        functools.partial(_proj_kernel, n_heads=n_heads, d_model=D),
        grid=(B, S // TOK_TILE),
        in_specs=[
            _tok_spec(D),
            _const_spec((1, D)),
            _const_spec(w_in.shape),
            _const_spec(w_pool_mix.shape),
            _const_spec((1, POOL_WIDTH)),
            _const_spec(w_br_pool.shape),
            _const_spec(w_gate.shape),
            _const_spec((1, 2 * D)),
        ],
        out_specs=[head_tile, head_tile, head_tile, _tok_spec(D), _tok_spec(D)],
        out_shape=[head_shape, head_shape, head_shape,
                   jax.ShapeDtypeStruct((B, S, D), BF16),
                   jax.ShapeDtypeStruct((B, S, D), BF16)],
        scratch_shapes=[pltpu.VMEM((HALO + TOK_TILE, POOL_WIDTH), F32),
                        pltpu.VMEM((len(POOL_WINDOWS), HALO + TOK_TILE, POOL_GROUP), F32)],
        compiler_params=pltpu.CompilerParams(
            dimension_semantics=("arbitrary", "arbitrary"),
            vmem_limit_bytes=VMEM_LIMIT_BYTES),
        name="proj_pool_gates",
    )(x, _as_row(g_pre_mix), w_in.astype(BF16), w_pool_mix.astype(BF16), _as_row(pool_scale),
      w_br_pool.astype(BF16), w_gate.astype(BF16), _as_row(b_gate))


def _attention(q, k, v):
    B, n_heads, n_blk = q.shape[:3]
    assert n_blk % ATT_QBLKS == 0
    q_spec = pl.BlockSpec((1, n_heads, ATT_QBLKS, ATT_BLK, HEAD_DIM),
                          lambda b, i: (b, 0, i, 0, 0))
    seq_spec = pl.BlockSpec((1, n_heads, n_blk, ATT_BLK, HEAD_DIM), lambda b, i: (b, 0, 0, 0, 0))
    sb_width = n_heads * HEAD_DIM
    return pl.pallas_call(
        _attn_kernel,
        grid=(B, n_blk // ATT_QBLKS),
        in_specs=[q_spec, seq_spec, seq_spec],
        out_specs=pl.BlockSpec((1, ATT_QBLKS * ATT_BLK, sb_width), lambda b, i: (b, i, 0)),
        out_shape=jax.ShapeDtypeStruct((B, n_blk * ATT_BLK, sb_width), BF16),
        scratch_shapes=[pltpu.VMEM((n_heads, ATT_BLK, 1), F32),
                        pltpu.VMEM((n_heads, ATT_BLK, HEAD_DIM), F32)],
        compiler_params=pltpu.CompilerParams(
            dimension_semantics=("parallel", "parallel"),
            vmem_limit_bytes=VMEM_LIMIT_BYTES),
        name="stickbreak_attn",
    )(q, k, v)


def _merge_mlp(x, o_sb, gy, gsb, w_br_sb, w_out, g_post_mix, g_pre_mlp, w_up, w_down,
               g_post_mlp):
    B, S, D = x.shape
    return pl.pallas_call(
        _out_kernel,
        grid=(B, S // TOK_TILE),
        in_specs=[
            _tok_spec(D),
            _tok_spec(o_sb.shape[-1]),
            _tok_spec(D),
            _tok_spec(D),
            _const_spec(w_br_sb.shape),
            _const_spec(w_out.shape),
            _const_spec((1, D)),
            _const_spec((1, D)),
            _const_spec(w_up.shape),
            _const_spec(w_down.shape),
            _const_spec((1, D)),
        ],
        out_specs=_tok_spec(D),
        out_shape=jax.ShapeDtypeStruct((B, S, D), x.dtype),
        compiler_params=pltpu.CompilerParams(
            dimension_semantics=("parallel", "parallel"),
            vmem_limit_bytes=VMEM_LIMIT_BYTES),
        name="merge_mlp",
    )(x, o_sb, gy, gsb, w_br_sb.astype(BF16), w_out.astype(BF16), _as_row(g_post_mix),
      _as_row(g_pre_mlp), w_up.astype(BF16), w_down.astype(BF16), _as_row(g_post_mlp))


def kernel(x, g_pre_mix, w_in, w_pool_mix, pool_scale, w_br_pool, w_br_sb, w_gate, b_gate,
           w_out, g_post_mix, g_pre_mlp, w_up, w_down, g_post_mlp):
    B, S, D = x.shape
    sb_width = w_br_sb.shape[0]
    assert w_in.shape[1] == POOL_WIDTH + 3 * sb_width and w_br_pool.shape[0] == POOL_WIDTH
    assert S % TOK_TILE == 0 and TOK_TILE % ATT_BLK == 0 and w_gate.shape[1] == 2 * D
    assert w_up.shape[1] % FF_CHUNK == 0

    q, k, v, gy, gsb = _project(x, g_pre_mix, w_in, w_pool_mix, pool_scale, w_br_pool,
                                w_gate, b_gate, sb_width // HEAD_DIM)
    o_sb = _attention(q, k, v)
    return _merge_mlp(x, o_sb, gy, gsb, w_br_sb, w_out, g_post_mix, g_pre_mlp, w_up, w_down,
                      g_post_mlp)
```

```python
import functools

import jax
import jax.numpy as jnp
from jax import lax
from jax.experimental import pallas as pl
from jax.experimental.pallas import tpu as pltpu

RMS_EPS = 1e-6
POOL_WINDOWS = (2, 4, 8, 16)
POOL_GROUP = 128
POOL_WIDTH = POOL_GROUP * len(POOL_WINDOWS)
HEAD_DIM = 64
HALO = 32
TOK_TILE = 512
ATT_BLK = 256
ATT_QBLKS = 2
FF_CHUNK = 1024
DEAD_LOG2 = -151.0
LOG2E = 1.4426950408889634

VMEM_LIMIT_BYTES = 52 * 1024 * 1024

BF16 = jnp.bfloat16
F32 = jnp.float32


def _rms_scale(x):
    return lax.rsqrt(jnp.mean(x * x, axis=-1, keepdims=True) + RMS_EPS)


def _const_spec(shape):
    zeros = (0,) * len(shape)
    return pl.BlockSpec(shape, lambda *_: zeros, pipeline_mode=pl.Buffered(1))


def _proj_kernel(x_ref, g_pre_ref, w_in_ref, w_pm_ref, pscale_ref, w_brp_ref,
                 w_gate_ref, b_gate_ref, *rest, n_heads, d_model, n_cast):
    cast_src, rest = rest[:n_cast], rest[n_cast:]
    q_ref, k_ref, v_ref, gy_ref, gsb_ref = rest[:5]
    cast_dst, (uext_ref, lvl_ref) = rest[5:5 + n_cast], rest[5 + n_cast:]
    s = pl.program_id(1)
    tm = x_ref.shape[1]
    sb_width = n_heads * HEAD_DIM

    @pl.when(jnp.logical_and(pl.program_id(0) == 0, s == 0))
    def _():
        uext_ref[tm:tm + HALO, :] = jnp.zeros((HALO, POOL_WIDTH), F32)

    hs = []
    for r in (pl.ds(0, tm // 2), pl.ds(tm // 2, tm // 2)):
        x = x_ref[0, r]
        hs.append((x * _rms_scale(x) * g_pre_ref[...]).astype(BF16))
    both = lambda w_ref: jnp.concatenate(
        [jnp.dot(h, w_ref[...], preferred_element_type=F32) for h in hs], axis=0)
    proj = both(w_in_ref)
    gates = jax.nn.sigmoid(both(w_gate_ref) + b_gate_ref[...])
    gsb_ref[0] = gates[:, d_model:].astype(BF16)

    q = proj[:, POOL_WIDTH:POOL_WIDTH + sb_width] * (HEAD_DIM ** -0.5 * LOG2E)
    k = proj[:, POOL_WIDTH + sb_width:POOL_WIDTH + 2 * sb_width]
    v = proj[:, POOL_WIDTH + 2 * sb_width:POOL_WIDTH + 3 * sb_width]
    for src, dst in ((q, q_ref), (k, k_ref), (v, v_ref)):
        for hh in range(n_heads):
            cols = src[:, hh * HEAD_DIM:(hh + 1) * HEAD_DIM].astype(BF16)
            for t in range(tm // ATT_BLK):
                dst[0, hh, t] = cols[t * ATT_BLK:(t + 1) * ATT_BLK]

    uext_ref[0:HALO, :] = jnp.where(s > 0, uext_ref[tm:tm + HALO, :], 0.0)

    u = proj[:, :POOL_WIDTH]
    uext_ref[HALO:HALO + tm, :] = u

    for src, dst in zip(cast_src, cast_dst):
        dst[...] = src[...].astype(BF16)

    pos = s * tm + lax.broadcasted_iota(jnp.int32, (tm, POOL_GROUP), 0)
    rows = HALO + tm
    mixed = []
    for g, w in enumerate(POOL_WINDOWS):
        c0, c1 = g * POOL_GROUP, (g + 1) * POOL_GROUP
        levels = w.bit_length() - 1
        start = HALO - 8 * (levels - 1)
        win = uext_ref[start:rows, c0:c1] + uext_ref[start - 1:rows - 1, c0:c1]
        for lvl in range(2, levels + 1):
            back = 1 << (lvl - 1)
            lvl_ref[g, start:rows, :] = win
            start += 8
            win = win[8:] + lvl_ref[g, start - back:rows - back, :]
        count = jnp.minimum(pos + 1, w).astype(F32)
        pooled = win / count - u[:, c0:c1]
        mixed.append(jnp.dot(pooled.astype(BF16), w_pm_ref[g], preferred_element_type=F32))
    y = jnp.concatenate(mixed, axis=-1) * pscale_ref[...]
    y_pool = jnp.dot(y.astype(BF16), w_brp_ref[...], preferred_element_type=F32)

    gy_ref[0] = (gates[:, :d_model] * y_pool).astype(BF16)


def _attn_kernel(q_ref, k_ref, v_ref, o_ref, carry_ref, acc_ref):
    n_heads, n_sub, blk, _ = q_ref.shape[1:]
    wide = 2 * blk

    def strict_lower(n):
        r = lax.broadcasted_iota(jnp.int32, (n, n), 0)
        c = lax.broadcasted_iota(jnp.int32, (n, n), 1)
        return jnp.where(r > c, 1.0, 0.0).astype(BF16)

    def scores(q, keys):
        return lax.dot_general(q, keys, (((1,), (1,)), ((), ())), preferred_element_type=F32)

    def log_terms(z):
        soft = jnp.log(1.0 + jnp.exp2(-jnp.abs(z))) * LOG2E
        lsz = jnp.minimum(z, 0.0) - soft
        return lsz, lsz - z

    tri = strict_lower(blk)
    row = lax.broadcasted_iota(jnp.int32, (blk, wide), 0)
    col = lax.broadcasted_iota(jnp.int32, (blk, wide), 1)
    causal = col < row + blk

    def first_tile(sub):
        i = pl.program_id(1) * n_sub + sub
        prev = jnp.maximum(i - 1, 0)
        lsz_all, l1m_all = [], []
        for hh in range(n_heads):
            keys = jnp.concatenate([k_ref[0, hh, prev], k_ref[0, hh, i]], axis=0)
            lsz, l1m = log_terms(scores(q_ref[0, hh, sub], keys))
            lsz_all.append(lsz)
            l1m_all.append(jnp.where(causal, l1m, 0.0))
        stack = lambda lo: jnp.concatenate([l1m[:, lo:lo + blk] for l1m in l1m_all],
                                           axis=0).astype(BF16)
        after_prev_all = jnp.dot(stack(0), tri, preferred_element_type=F32)
        after_diag_all = jnp.dot(stack(blk), tri, preferred_element_type=F32)
        carries, accs = [], []
        for hh in range(n_heads):
            l1m = l1m_all[hh]
            after_prev = after_prev_all[hh * blk:(hh + 1) * blk]
            after_diag = after_diag_all[hh * blk:(hh + 1) * blk]
            total_diag = after_diag[:, 0:1] + l1m[:, blk:blk + 1]
            after = jnp.concatenate([after_prev + total_diag, after_diag], axis=1)
            a = jnp.where(causal, jnp.exp2(lsz_all[hh] + after), 0.0)
            v_prev = v_ref[0, hh, prev]
            if sub == 0:
                v_prev = jnp.where(i > 0, v_prev, jnp.zeros((), BF16))
            vals = jnp.concatenate([v_prev, v_ref[0, hh, i]], axis=0)
            accs.append(jnp.dot(a.astype(BF16), vals, preferred_element_type=F32))
            carries.append(after_prev[:, 0:1] + total_diag + l1m[:, 0:1])
        o_ref[0, sub * blk:(sub + 1) * blk, :] = jnp.concatenate(accs, axis=-1).astype(BF16)
        return i, carries, accs

    def any_alive(carries):
        return jnp.max(functools.reduce(jnp.maximum, carries)) > DEAD_LOG2

    def earlier_blocks(sub, i, carries, accs):
        @pl.when(jnp.logical_and(i > 1, any_alive(carries)))
        def _():
            for hh in range(n_heads):
                carry_ref[hh] = carries[hh]
                acc_ref[hh] = accs[hh]

            def step(loop):
                it, _ = loop
                j = i - 2 - it
                new = []
                for hh in range(n_heads):
                    lsz, l1m = log_terms(scores(q_ref[0, hh, sub], k_ref[0, hh, j]))
                    after = jnp.dot(l1m.astype(BF16), tri, preferred_element_type=F32)
                    a = jnp.exp2(lsz + after + carry_ref[hh])
                    acc_ref[hh] += jnp.dot(a.astype(BF16), v_ref[0, hh, j],
                                           preferred_element_type=F32)
                    new.append(carry_ref[hh] + after[:, 0:1] + l1m[:, 0:1])
                    carry_ref[hh] = new[-1]
                return it + 1, any_alive(new)

            lax.while_loop(lambda loop: jnp.logical_and(loop[0] < i - 1, loop[1]),
                           step, (jnp.int32(0), True))
            o_ref[0, sub * blk:(sub + 1) * blk, :] = jnp.concatenate(
                [acc_ref[hh] for hh in range(n_heads)], axis=-1).astype(BF16)

    firsts = [first_tile(sub) for sub in range(n_sub)]
    for sub, (i, carries, accs) in enumerate(firsts):
        earlier_blocks(sub, i, carries, accs)


def _out_kernel(x_ref, osb_ref, gy_ref, gsb_ref, w_brs_ref, w_out_ref, g_post_mix_ref,
                g_pre_mlp_ref, w_up_ref, w_down_ref, g_post_mlp_ref, o_ref):
    def mixed(r):
        y_sb = jnp.dot(osb_ref[0, r], w_brs_ref[...], preferred_element_type=F32)
        merged = gy_ref[0, r].astype(F32) + gsb_ref[0, r].astype(F32) * y_sb
        return jnp.dot(merged.astype(BF16), w_out_ref[...], preferred_element_type=F32)

    def normed(r, mix):
        x1 = x_ref[0, r] + mix * _rms_scale(mix) * g_post_mix_ref[...]
        return x1, (x1 * _rms_scale(x1) * g_pre_mlp_ref[...]).astype(BF16)

    def mlp(h2):
        ff = None
        for c0 in range(0, w_up_ref.shape[1], FF_CHUNK):
            up = jnp.dot(h2, w_up_ref[:, c0:c0 + FF_CHUNK], preferred_element_type=F32)
            act = jnp.square(jnp.maximum(up, 0.0)).astype(BF16)
            part = jnp.dot(act, w_down_ref[c0:c0 + FF_CHUNK, :], preferred_element_type=F32)
            ff = part if ff is None else ff + part
        return ff

    half = x_ref.shape[1] // 2
    halves = (pl.ds(0, half), pl.ds(half, half))
    mixes = [mixed(r) for r in halves]
    normeds = [normed(r, mix) for r, mix in zip(halves, mixes)]
    for r, (x1, h2) in zip(halves, normeds):
        ff = mlp(h2)
        o_ref[0, r] = x1 + ff * _rms_scale(ff) * g_post_mlp_ref[...]


def _as_row(a):
    return a.reshape(1, -1).astype(F32)


def _tok_spec(width):
    return pl.BlockSpec((1, TOK_TILE, width), lambda b, s: (b, s, 0))


def _project(x, g_pre_mix, w_in, w_pool_mix, pool_scale, w_br_pool, w_gate, b_gate, n_heads,
             later_weights):
    B, S, D = x.shape
    sub = TOK_TILE // ATT_BLK
    n_tiles = S // TOK_TILE
    n_steps = B * n_tiles
    assert all(w.ndim == 2 and w.shape[0] % (16 * n_steps) == 0 for w in later_weights)
    slice_specs = [pl.BlockSpec((w.shape[0] // n_steps, w.shape[1]),
                                lambda b, s: (b * n_tiles + s, 0)) for w in later_weights]
    head_tile = pl.BlockSpec((1, n_heads, sub, ATT_BLK, HEAD_DIM), lambda b, s: (b, 0, s, 0, 0))
    head_shape = jax.ShapeDtypeStruct((B, n_heads, S // ATT_BLK, ATT_BLK, HEAD_DIM), BF16)
    return pl.pallas_call(
        functools.partial(_proj_kernel, n_heads=n_heads, d_model=D,
                          n_cast=len(later_weights)),
        grid=(B, n_tiles),
        in_specs=[
            _tok_spec(D),
            _const_spec((1, D)),
            _const_spec(w_in.shape),
            _const_spec(w_pool_mix.shape),
            _const_spec((1, POOL_WIDTH)),
            _const_spec(w_br_pool.shape),
            _const_spec(w_gate.shape),
            _const_spec((1, 2 * D)),
            *slice_specs,
        ],
        out_specs=[head_tile, head_tile, head_tile, _tok_spec(D), _tok_spec(D), *slice_specs],
        out_shape=[head_shape, head_shape, head_shape,
                   jax.ShapeDtypeStruct((B, S, D), BF16),
                   jax.ShapeDtypeStruct((B, S, D), BF16),
                   *[jax.ShapeDtypeStruct(w.shape, BF16) for w in later_weights]],
        scratch_shapes=[pltpu.VMEM((HALO + TOK_TILE, POOL_WIDTH), F32),
                        pltpu.VMEM((len(POOL_WINDOWS), HALO + TOK_TILE, POOL_GROUP), F32)],
        compiler_params=pltpu.CompilerParams(
            dimension_semantics=("arbitrary", "arbitrary"),
            vmem_limit_bytes=VMEM_LIMIT_BYTES),
        name="proj_pool_gates",
    )(x, _as_row(g_pre_mix), w_in.astype(BF16), w_pool_mix.astype(BF16), _as_row(pool_scale),
      w_br_pool.astype(BF16), w_gate.astype(BF16), _as_row(b_gate), *later_weights)


def _attention(q, k, v):
    B, n_heads, n_blk = q.shape[:3]
    assert n_blk % ATT_QBLKS == 0
    q_spec = pl.BlockSpec((1, n_heads, ATT_QBLKS, ATT_BLK, HEAD_DIM),
                          lambda b, i: (b, 0, i, 0, 0))
    seq_spec = pl.BlockSpec((1, n_heads, n_blk, ATT_BLK, HEAD_DIM), lambda b, i: (b, 0, 0, 0, 0))
    sb_width = n_heads * HEAD_DIM
    return pl.pallas_call(
        _attn_kernel,
        grid=(B, n_blk // ATT_QBLKS),
        in_specs=[q_spec, seq_spec, seq_spec],
        out_specs=pl.BlockSpec((1, ATT_QBLKS * ATT_BLK, sb_width), lambda b, i: (b, i, 0)),
        out_shape=jax.ShapeDtypeStruct((B, n_blk * ATT_BLK, sb_width), BF16),
        scratch_shapes=[pltpu.VMEM((n_heads, ATT_BLK, 1), F32),
                        pltpu.VMEM((n_heads, ATT_BLK, HEAD_DIM), F32)],
        compiler_params=pltpu.CompilerParams(
            dimension_semantics=("parallel", "parallel"),
            vmem_limit_bytes=VMEM_LIMIT_BYTES),
        name="stickbreak_attn",
    )(q, k, v)


def _merge_mlp(x, o_sb, gy, gsb, w_br_sb, w_out, g_post_mix, g_pre_mlp, w_up, w_down,
               g_post_mlp):
    B, S, D = x.shape
    return pl.pallas_call(
        _out_kernel,
        grid=(B, S // TOK_TILE),
        in_specs=[
            _tok_spec(D),
            _tok_spec(o_sb.shape[-1]),
            _tok_spec(D),
            _tok_spec(D),
            _const_spec(w_br_sb.shape),
            _const_spec(w_out.shape),
            _const_spec((1, D)),
            _const_spec((1, D)),
            _const_spec(w_up.shape),
            _const_spec(w_down.shape),
            _const_spec((1, D)),
        ],
        out_specs=_tok_spec(D),
        out_shape=jax.ShapeDtypeStruct((B, S, D), x.dtype),
        compiler_params=pltpu.CompilerParams(
            dimension_semantics=("parallel", "parallel"),
            vmem_limit_bytes=VMEM_LIMIT_BYTES),
        name="merge_mlp",
    )(x, o_sb, gy, gsb, w_br_sb, w_out, _as_row(g_post_mix), _as_row(g_pre_mlp), w_up, w_down,
      _as_row(g_post_mlp))


def kernel(x, g_pre_mix, w_in, w_pool_mix, pool_scale, w_br_pool, w_br_sb, w_gate, b_gate,
           w_out, g_post_mix, g_pre_mlp, w_up, w_down, g_post_mlp):
    B, S, D = x.shape
    sb_width = w_br_sb.shape[0]
    assert w_in.shape[1] == POOL_WIDTH + 3 * sb_width and w_br_pool.shape[0] == POOL_WIDTH
    assert S % TOK_TILE == 0 and TOK_TILE % ATT_BLK == 0 and w_gate.shape[1] == 2 * D
    assert w_up.shape[1] % FF_CHUNK == 0

    q, k, v, gy, gsb, w_br_sb, w_out, w_up, w_down = _project(
        x, g_pre_mix, w_in, w_pool_mix, pool_scale, w_br_pool, w_gate, b_gate,
        sb_width // HEAD_DIM, later_weights=(w_br_sb, w_out, w_up, w_down))
    o_sb = _attention(q, k, v)
    return _merge_mlp(x, o_sb, gy, gsb, w_br_sb, w_out, g_post_mix, g_pre_mlp, w_up, w_down,
                      g_post_mlp)
```

```python
import functools

import jax
import jax.numpy as jnp
from jax import lax
from jax.experimental import pallas as pl
from jax.experimental.pallas import tpu as pltpu

RMS_EPS = 1e-6
POOL_WINDOWS = (2, 4, 8, 16)
POOL_GROUP = 128
POOL_WIDTH = POOL_GROUP * len(POOL_WINDOWS)
HEAD_DIM = 64
SUBLANES = 8
BF16_ROWS = 16
HALO = 4 * SUBLANES
TOK_TILE = 512
ATT_BLK = 256
ATT_QBLKS = 2
FF_CHUNK = 1024
DEAD_LOG2 = -151.0
LOG2E = 1.4426950408889634

VMEM_LIMIT_BYTES = 52 * 1024 * 1024

BF16 = jnp.bfloat16
F32 = jnp.float32


def _rms_scale(x):
    return lax.rsqrt(jnp.mean(x * x, axis=-1, keepdims=True) + RMS_EPS)


def _const_spec(shape):
    zeros = (0,) * len(shape)
    return pl.BlockSpec(shape, lambda *_: zeros, pipeline_mode=pl.Buffered(1))


def _proj_kernel(x_ref, g_pre_ref, w_in_ref, w_pm_ref, pscale_ref, w_brp_ref,
                 w_gate_ref, b_gate_ref, *rest, n_heads, d_model, n_cast):
    cast_src, rest = rest[:n_cast], rest[n_cast:]
    q_ref, k_ref, v_ref, gy_ref, gsb_ref = rest[:5]
    cast_dst, (uext_ref, lvl_ref) = rest[5:5 + n_cast], rest[5 + n_cast:]
    s = pl.program_id(1)
    tm = x_ref.shape[1]
    sb_width = n_heads * HEAD_DIM

    @pl.when(jnp.logical_and(pl.program_id(0) == 0, s == 0))
    def _():
        uext_ref[tm:tm + HALO, :] = jnp.zeros((HALO, POOL_WIDTH), F32)

    hs = []
    for r in (pl.ds(0, tm // 2), pl.ds(tm // 2, tm // 2)):
        x = x_ref[0, r]
        hs.append((x * _rms_scale(x) * g_pre_ref[...]).astype(BF16))
    both = lambda w_ref: jnp.concatenate(
        [jnp.dot(h, w_ref[...], preferred_element_type=F32) for h in hs], axis=0)
    proj = both(w_in_ref)
    gates = jax.nn.sigmoid(both(w_gate_ref) + b_gate_ref[...])
    gsb_ref[0] = gates[:, d_model:].astype(BF16)

    q = proj[:, POOL_WIDTH:POOL_WIDTH + sb_width] * (HEAD_DIM ** -0.5 * LOG2E)
    k = proj[:, POOL_WIDTH + sb_width:POOL_WIDTH + 2 * sb_width]
    v = proj[:, POOL_WIDTH + 2 * sb_width:POOL_WIDTH + 3 * sb_width]
    for src, dst in ((q, q_ref), (k, k_ref), (v, v_ref)):
        for hh in range(n_heads):
            cols = src[:, hh * HEAD_DIM:(hh + 1) * HEAD_DIM].astype(BF16)
            for t in range(tm // ATT_BLK):
                dst[0, hh, t] = cols[t * ATT_BLK:(t + 1) * ATT_BLK]

    uext_ref[0:HALO, :] = jnp.where(s > 0, uext_ref[tm:tm + HALO, :], 0.0)

    u = proj[:, :POOL_WIDTH]
    uext_ref[HALO:HALO + tm, :] = u

    for src, dst in zip(cast_src, cast_dst):
        dst[...] = src[...].astype(BF16)

    pos = s * tm + lax.broadcasted_iota(jnp.int32, (tm, POOL_GROUP), 0)
    rows = HALO + tm
    mixed = []
    for g, w in enumerate(POOL_WINDOWS):
        c0, c1 = g * POOL_GROUP, (g + 1) * POOL_GROUP
        levels = w.bit_length() - 1
        start = HALO - SUBLANES * (levels - 1)
        win = uext_ref[start:rows, c0:c1] + uext_ref[start - 1:rows - 1, c0:c1]
        for lvl in range(2, levels + 1):
            back = 1 << (lvl - 1)
            lvl_ref[g, start:rows, :] = win
            start += SUBLANES
            win = win[SUBLANES:] + lvl_ref[g, start - back:rows - back, :]
        count = jnp.minimum(pos + 1, w).astype(F32)
        pooled = win / count - u[:, c0:c1]
        mixed.append(jnp.dot(pooled.astype(BF16), w_pm_ref[g], preferred_element_type=F32))
    y = jnp.concatenate(mixed, axis=-1) * pscale_ref[...]
    y_pool = jnp.dot(y.astype(BF16), w_brp_ref[...], preferred_element_type=F32)

    gy_ref[0] = (gates[:, :d_model] * y_pool).astype(BF16)


def _attn_kernel(q_ref, k_ref, v_ref, o_ref, carry_ref, acc_ref):
    n_heads, n_sub, blk, _ = q_ref.shape[1:]
    wide = 2 * blk

    def strict_lower(n):
        r = lax.broadcasted_iota(jnp.int32, (n, n), 0)
        c = lax.broadcasted_iota(jnp.int32, (n, n), 1)
        return jnp.where(r > c, 1.0, 0.0).astype(BF16)

    def scores(q, keys):
        return lax.dot_general(q, keys, (((1,), (1,)), ((), ())), preferred_element_type=F32)

    def log_terms(z):
        soft = jnp.log(1.0 + jnp.exp2(-jnp.abs(z))) * LOG2E
        lsz = jnp.minimum(z, 0.0) - soft
        return lsz, lsz - z

    tri = strict_lower(blk)
    row = lax.broadcasted_iota(jnp.int32, (blk, wide), 0)
    col = lax.broadcasted_iota(jnp.int32, (blk, wide), 1)
    causal = col < row + blk

    def first_tile(sub):
        i = pl.program_id(1) * n_sub + sub
        prev = jnp.maximum(i - 1, 0)
        lsz_all, l1m_all = [], []
        for hh in range(n_heads):
            keys = jnp.concatenate([k_ref[0, hh, prev], k_ref[0, hh, i]], axis=0)
            lsz, l1m = log_terms(scores(q_ref[0, hh, sub], keys))
            lsz_all.append(lsz)
            l1m_all.append(jnp.where(causal, l1m, 0.0))
        stack = lambda lo: jnp.concatenate([l1m[:, lo:lo + blk] for l1m in l1m_all],
                                           axis=0).astype(BF16)
        after_prev_all = jnp.dot(stack(0), tri, preferred_element_type=F32)
        after_diag_all = jnp.dot(stack(blk), tri, preferred_element_type=F32)
        carries, accs = [], []
        for hh in range(n_heads):
            l1m = l1m_all[hh]
            after_prev = after_prev_all[hh * blk:(hh + 1) * blk]
            after_diag = after_diag_all[hh * blk:(hh + 1) * blk]
            total_diag = after_diag[:, 0:1] + l1m[:, blk:blk + 1]
            after = jnp.concatenate([after_prev + total_diag, after_diag], axis=1)
            a = jnp.where(causal, jnp.exp2(lsz_all[hh] + after), 0.0)
            v_prev = v_ref[0, hh, prev]
            if sub == 0:
                v_prev = jnp.where(i > 0, v_prev, jnp.zeros((), BF16))
            vals = jnp.concatenate([v_prev, v_ref[0, hh, i]], axis=0)
            accs.append(jnp.dot(a.astype(BF16), vals, preferred_element_type=F32))
            carries.append(after_prev[:, 0:1] + total_diag + l1m[:, 0:1])
        o_ref[0, sub * blk:(sub + 1) * blk, :] = jnp.concatenate(accs, axis=-1).astype(BF16)
        return i, carries, accs

    def any_alive(carries):
        return jnp.max(functools.reduce(jnp.maximum, carries)) > DEAD_LOG2

    def earlier_blocks(sub, i, carries, accs):
        @pl.when(jnp.logical_and(i > 1, any_alive(carries)))
        def _():
            for hh in range(n_heads):
                carry_ref[hh] = carries[hh]
                acc_ref[hh] = accs[hh]

            def step(loop):
                it, _ = loop
                j = i - 2 - it
                new = []
                for hh in range(n_heads):
                    lsz, l1m = log_terms(scores(q_ref[0, hh, sub], k_ref[0, hh, j]))
                    after = jnp.dot(l1m.astype(BF16), tri, preferred_element_type=F32)
                    a = jnp.exp2(lsz + after + carry_ref[hh])
                    acc_ref[hh] += jnp.dot(a.astype(BF16), v_ref[0, hh, j],
                                           preferred_element_type=F32)
                    new.append(carry_ref[hh] + after[:, 0:1] + l1m[:, 0:1])
                    carry_ref[hh] = new[-1]
                return it + 1, any_alive(new)

            lax.while_loop(lambda loop: jnp.logical_and(loop[0] < i - 1, loop[1]),
                           step, (jnp.int32(0), True))
            o_ref[0, sub * blk:(sub + 1) * blk, :] = jnp.concatenate(
                [acc_ref[hh] for hh in range(n_heads)], axis=-1).astype(BF16)

    firsts = [first_tile(sub) for sub in range(n_sub)]
    for sub, (i, carries, accs) in enumerate(firsts):
        earlier_blocks(sub, i, carries, accs)


def _out_kernel(x_ref, osb_ref, gy_ref, gsb_ref, w_brs_ref, w_out_ref, g_post_mix_ref,
                g_pre_mlp_ref, w_up_ref, w_down_ref, g_post_mlp_ref, o_ref):
    def mixed(r):
        y_sb = jnp.dot(osb_ref[0, r], w_brs_ref[...], preferred_element_type=F32)
        merged = gy_ref[0, r].astype(F32) + gsb_ref[0, r].astype(F32) * y_sb
        return jnp.dot(merged.astype(BF16), w_out_ref[...], preferred_element_type=F32)

    def normed(r, mix):
        x1 = x_ref[0, r] + mix * _rms_scale(mix) * g_post_mix_ref[...]
        return x1, (x1 * _rms_scale(x1) * g_pre_mlp_ref[...]).astype(BF16)

    def mlp(h2):
        ff = None
        for c0 in range(0, w_up_ref.shape[1], FF_CHUNK):
            up = jnp.dot(h2, w_up_ref[:, c0:c0 + FF_CHUNK], preferred_element_type=F32)
            act = jnp.square(jnp.maximum(up, 0.0)).astype(BF16)
            part = jnp.dot(act, w_down_ref[c0:c0 + FF_CHUNK, :], preferred_element_type=F32)
            ff = part if ff is None else ff + part
        return ff

    half = x_ref.shape[1] // 2
    halves = (pl.ds(0, half), pl.ds(half, half))
    mixes = [mixed(r) for r in halves]
    normeds = [normed(r, mix) for r, mix in zip(halves, mixes)]
    for r, (x1, h2) in zip(halves, normeds):
        ff = mlp(h2)
        o_ref[0, r] = x1 + ff * _rms_scale(ff) * g_post_mlp_ref[...]


def _as_row(a):
    return a.reshape(1, -1).astype(F32)


def _tok_spec(width):
    return pl.BlockSpec((1, TOK_TILE, width), lambda b, s: (b, s, 0))


def _project(x, g_pre_mix, w_in, w_pool_mix, pool_scale, w_br_pool, w_gate, b_gate, n_heads,
             later_weights):
    B, S, D = x.shape
    sub = TOK_TILE // ATT_BLK
    n_tiles = S // TOK_TILE
    n_steps = B * n_tiles
    assert all(w.ndim == 2 and w.shape[0] % (BF16_ROWS * n_steps) == 0 for w in later_weights)
    slice_specs = [pl.BlockSpec((w.shape[0] // n_steps, w.shape[1]),
                                lambda b, s: (b * n_tiles + s, 0)) for w in later_weights]
    head_tile = pl.BlockSpec((1, n_heads, sub, ATT_BLK, HEAD_DIM), lambda b, s: (b, 0, s, 0, 0))
    head_shape = jax.ShapeDtypeStruct((B, n_heads, S // ATT_BLK, ATT_BLK, HEAD_DIM), BF16)
    return pl.pallas_call(
        functools.partial(_proj_kernel, n_heads=n_heads, d_model=D,
                          n_cast=len(later_weights)),
        grid=(B, n_tiles),
        in_specs=[
            _tok_spec(D),
            _const_spec((1, D)),
            _const_spec(w_in.shape),
            _const_spec(w_pool_mix.shape),
            _const_spec((1, POOL_WIDTH)),
            _const_spec(w_br_pool.shape),
            _const_spec(w_gate.shape),
            _const_spec((1, 2 * D)),
            *slice_specs,
        ],
        out_specs=[head_tile, head_tile, head_tile, _tok_spec(D), _tok_spec(D), *slice_specs],
        out_shape=[head_shape, head_shape, head_shape,
                   jax.ShapeDtypeStruct((B, S, D), BF16),
                   jax.ShapeDtypeStruct((B, S, D), BF16),
                   *[jax.ShapeDtypeStruct(w.shape, BF16) for w in later_weights]],
        scratch_shapes=[pltpu.VMEM((HALO + TOK_TILE, POOL_WIDTH), F32),
                        pltpu.VMEM((len(POOL_WINDOWS), HALO + TOK_TILE, POOL_GROUP), F32)],
        compiler_params=pltpu.CompilerParams(
            dimension_semantics=("arbitrary", "arbitrary"),
            vmem_limit_bytes=VMEM_LIMIT_BYTES),
        name="proj_pool_gates",
    )(x, _as_row(g_pre_mix), w_in.astype(BF16), w_pool_mix.astype(BF16), _as_row(pool_scale),
      w_br_pool.astype(BF16), w_gate.astype(BF16), _as_row(b_gate), *later_weights)


def _attention(q, k, v):
    B, n_heads, n_blk = q.shape[:3]
    assert n_blk % ATT_QBLKS == 0
    q_spec = pl.BlockSpec((1, n_heads, ATT_QBLKS, ATT_BLK, HEAD_DIM),
                          lambda b, i: (b, 0, i, 0, 0))
    seq_spec = pl.BlockSpec((1, n_heads, n_blk, ATT_BLK, HEAD_DIM), lambda b, i: (b, 0, 0, 0, 0))
    sb_width = n_heads * HEAD_DIM
    return pl.pallas_call(
        _attn_kernel,
        grid=(B, n_blk // ATT_QBLKS),
        in_specs=[q_spec, seq_spec, seq_spec],
        out_specs=pl.BlockSpec((1, ATT_QBLKS * ATT_BLK, sb_width), lambda b, i: (b, i, 0)),
        out_shape=jax.ShapeDtypeStruct((B, n_blk * ATT_BLK, sb_width), BF16),
        scratch_shapes=[pltpu.VMEM((n_heads, ATT_BLK, 1), F32),
                        pltpu.VMEM((n_heads, ATT_BLK, HEAD_DIM), F32)],
        compiler_params=pltpu.CompilerParams(
            dimension_semantics=("parallel", "parallel"),
            vmem_limit_bytes=VMEM_LIMIT_BYTES),
        name="stickbreak_attn",
    )(q, k, v)


def _merge_mlp(x, o_sb, gy, gsb, w_br_sb, w_out, g_post_mix, g_pre_mlp, w_up, w_down,
               g_post_mlp):
    B, S, D = x.shape
    return pl.pallas_call(
        _out_kernel,
        grid=(B, S // TOK_TILE),
        in_specs=[
            _tok_spec(D),
            _tok_spec(o_sb.shape[-1]),
            _tok_spec(D),
            _tok_spec(D),
            _const_spec(w_br_sb.shape),
            _const_spec(w_out.shape),
            _const_spec((1, D)),
            _const_spec((1, D)),
            _const_spec(w_up.shape),
            _const_spec(w_down.shape),
            _const_spec((1, D)),
        ],
        out_specs=_tok_spec(D),
        out_shape=jax.ShapeDtypeStruct((B, S, D), x.dtype),
        compiler_params=pltpu.CompilerParams(
            dimension_semantics=("parallel", "parallel"),
            vmem_limit_bytes=VMEM_LIMIT_BYTES),
        name="merge_mlp",
    )(x, o_sb, gy, gsb, w_br_sb, w_out, _as_row(g_post_mix), _as_row(g_pre_mlp), w_up, w_down,
      _as_row(g_post_mlp))


def kernel(x, g_pre_mix, w_in, w_pool_mix, pool_scale, w_br_pool, w_br_sb, w_gate, b_gate,
           w_out, g_post_mix, g_pre_mlp, w_up, w_down, g_post_mlp):
    B, S, D = x.shape
    sb_width = w_br_sb.shape[0]
    assert w_in.shape[1] == POOL_WIDTH + 3 * sb_width and w_br_pool.shape[0] == POOL_WIDTH
    assert S % TOK_TILE == 0 and TOK_TILE % ATT_BLK == 0 and w_gate.shape[1] == 2 * D
    assert w_up.shape[1] % FF_CHUNK == 0

    q, k, v, gy, gsb, w_br_sb, w_out, w_up, w_down = _project(
        x, g_pre_mix, w_in, w_pool_mix, pool_scale, w_br_pool, w_gate, b_gate,
        sb_width // HEAD_DIM, later_weights=(w_br_sb, w_out, w_up, w_down))
    o_sb = _attention(q, k, v)
    return _merge_mlp(x, o_sb, gy, gsb, w_br_sb, w_out, g_post_mix, g_pre_mlp, w_up, w_down,
                      g_post_mlp)
```

```python
import functools

import jax
import jax.numpy as jnp
from jax import lax
from jax.experimental import pallas as pl
from jax.experimental.pallas import tpu as pltpu

RMS_EPS = 1e-6
POOL_WINDOWS = (2, 4, 8, 16)
POOL_GROUP = 128
POOL_WIDTH = POOL_GROUP * len(POOL_WINDOWS)
HEAD_DIM = 64
SUBLANES = 8
BF16_ROWS = 16
HALO = 4 * SUBLANES
TOK_TILE = 512
ATT_BLK = 256
ATT_QBLKS = 2
FF_CHUNK = 1024
DEAD_LOG2 = -151.0
LOG2E = 1.4426950408889634

VMEM_LIMIT_BYTES = 52 * 1024 * 1024

BF16 = jnp.bfloat16
F32 = jnp.float32


def _rms_scale(x):
    return lax.rsqrt(jnp.mean(x * x, axis=-1, keepdims=True) + RMS_EPS)


def _const_spec(shape):
    zeros = (0,) * len(shape)
    return pl.BlockSpec(shape, lambda *_: zeros, pipeline_mode=pl.Buffered(1))


def _proj_kernel(x_ref, g_pre_ref, w_in_ref, w_pm_ref, pscale_ref,
                 w_gate_ref, b_gate_ref, *rest, n_heads, n_cast):
    cast_src, rest = rest[:n_cast], rest[n_cast:]
    q_ref, k_ref, v_ref, y_ref, gates_ref = rest[:5]
    cast_dst, (uext_ref, lvl_ref) = rest[5:5 + n_cast], rest[5 + n_cast:]
    s = pl.program_id(1)
    tm = x_ref.shape[1]
    sb_width = n_heads * HEAD_DIM

    @pl.when(jnp.logical_and(pl.program_id(0) == 0, s == 0))
    def _():
        uext_ref[tm:tm + HALO, :] = jnp.zeros((HALO, POOL_WIDTH), F32)

    hs = []
    for r in (pl.ds(0, tm // 2), pl.ds(tm // 2, tm // 2)):
        x = x_ref[0, r]
        hs.append((x * _rms_scale(x) * g_pre_ref[...]).astype(BF16))
    both = lambda w_ref: jnp.concatenate(
        [jnp.dot(h, w_ref[...], preferred_element_type=F32) for h in hs], axis=0)
    proj = both(w_in_ref)
    gates = jax.nn.sigmoid(both(w_gate_ref) + b_gate_ref[...])
    gates_ref[0] = gates.astype(BF16)

    q = proj[:, POOL_WIDTH:POOL_WIDTH + sb_width] * (HEAD_DIM ** -0.5 * LOG2E)
    k = proj[:, POOL_WIDTH + sb_width:POOL_WIDTH + 2 * sb_width]
    v = proj[:, POOL_WIDTH + 2 * sb_width:POOL_WIDTH + 3 * sb_width]
    for src, dst in ((q, q_ref), (k, k_ref), (v, v_ref)):
        for hh in range(n_heads):
            cols = src[:, hh * HEAD_DIM:(hh + 1) * HEAD_DIM].astype(BF16)
            for t in range(tm // ATT_BLK):
                dst[0, hh, t] = cols[t * ATT_BLK:(t + 1) * ATT_BLK]

    uext_ref[0:HALO, :] = jnp.where(s > 0, uext_ref[tm:tm + HALO, :], 0.0)

    u = proj[:, :POOL_WIDTH]
    uext_ref[HALO:HALO + tm, :] = u

    for src, dst in zip(cast_src, cast_dst):
        dst[...] = src[...].astype(BF16)

    pos = s * tm + lax.broadcasted_iota(jnp.int32, (tm, POOL_GROUP), 0)
    rows = HALO + tm
    mixed = []
    for g, w in enumerate(POOL_WINDOWS):
        c0, c1 = g * POOL_GROUP, (g + 1) * POOL_GROUP
        levels = w.bit_length() - 1
        start = HALO - SUBLANES * (levels - 1)
        win = uext_ref[start:rows, c0:c1] + uext_ref[start - 1:rows - 1, c0:c1]
        for lvl in range(2, levels + 1):
            back = 1 << (lvl - 1)
            lvl_ref[g, start:rows, :] = win
            start += SUBLANES
            win = win[SUBLANES:] + lvl_ref[g, start - back:rows - back, :]
        count = jnp.minimum(pos + 1, w).astype(F32)
        pooled = win / count - u[:, c0:c1]
        mixed.append(jnp.dot(pooled.astype(BF16), w_pm_ref[g], preferred_element_type=F32))
    y_ref[0] = (jnp.concatenate(mixed, axis=-1) * pscale_ref[...]).astype(BF16)


def _attn_kernel(q_ref, k_ref, v_ref, y_ref, gates_ref, w_brp_ref, o_ref, gy_ref, carry_ref,
                 acc_ref):
    n_heads, n_sub, blk, _ = q_ref.shape[1:]
    wide = 2 * blk

    def strict_lower(n):
        r = lax.broadcasted_iota(jnp.int32, (n, n), 0)
        c = lax.broadcasted_iota(jnp.int32, (n, n), 1)
        return jnp.where(r > c, 1.0, 0.0).astype(BF16)

    def scores(q, keys):
        return lax.dot_general(q, keys, (((1,), (1,)), ((), ())), preferred_element_type=F32)

    def log_terms(z):
        soft = jnp.log(1.0 + jnp.exp2(-jnp.abs(z))) * LOG2E
        lsz = jnp.minimum(z, 0.0) - soft
        return lsz, lsz - z

    tri = strict_lower(blk)
    row = lax.broadcasted_iota(jnp.int32, (blk, wide), 0)
    col = lax.broadcasted_iota(jnp.int32, (blk, wide), 1)
    causal = col < row + blk

    def block_ids(sub):
        i = pl.program_id(1) * n_sub + sub
        return i, jnp.maximum(i - 1, 0)

    def stage_scores(sub):
        i, prev = block_ids(sub)
        lsz_all, l1m_all = [], []
        for hh in range(n_heads):
            keys = jnp.concatenate([k_ref[0, hh, prev], k_ref[0, hh, i]], axis=0)
            lsz, l1m = log_terms(scores(q_ref[0, hh, sub], keys))
            lsz_all.append(lsz)
            l1m_all.append(jnp.where(causal, l1m, 0.0))
        return lsz_all, l1m_all

    def stage_pool_branch(sub):
        rows = pl.ds(sub * blk, blk)
        y_pool = jnp.dot(y_ref[0, rows], w_brp_ref[...], preferred_element_type=F32)
        gate_pool = gates_ref[0, rows, :y_pool.shape[1]].astype(F32)
        gy_ref[0, rows] = (gate_pool * y_pool).astype(BF16)

    def stage_sums(l1m_all):
        stack = lambda lo: jnp.concatenate([l1m[:, lo:lo + blk] for l1m in l1m_all],
                                           axis=0).astype(BF16)
        return (jnp.dot(stack(0), tri, preferred_element_type=F32),
                jnp.dot(stack(blk), tri, preferred_element_type=F32))

    def stage_values(sub, lsz_all, l1m_all, after_prev_all, after_diag_all):
        i, prev = block_ids(sub)
        carries, accs = [], []
        for hh in range(n_heads):
            l1m = l1m_all[hh]
            after_prev = after_prev_all[hh * blk:(hh + 1) * blk]
            after_diag = after_diag_all[hh * blk:(hh + 1) * blk]
            total_diag = after_diag[:, 0:1] + l1m[:, blk:blk + 1]
            after = jnp.concatenate([after_prev + total_diag, after_diag], axis=1)
            a = jnp.where(causal, jnp.exp2(lsz_all[hh] + after), 0.0)
            v_prev = v_ref[0, hh, prev]
            if sub == 0:
                v_prev = jnp.where(i > 0, v_prev, jnp.zeros((), BF16))
            vals = jnp.concatenate([v_prev, v_ref[0, hh, i]], axis=0)
            accs.append(jnp.dot(a.astype(BF16), vals, preferred_element_type=F32))
            carries.append(after_prev[:, 0:1] + total_diag + l1m[:, 0:1])
        o_ref[0, sub * blk:(sub + 1) * blk, :] = jnp.concatenate(accs, axis=-1).astype(BF16)
        return i, carries, accs

    def any_alive(carries):
        return jnp.max(functools.reduce(jnp.maximum, carries)) > DEAD_LOG2

    def earlier_blocks(sub, i, carries, accs):
        @pl.when(jnp.logical_and(i > 1, any_alive(carries)))
        def _():
            for hh in range(n_heads):
                carry_ref[hh] = carries[hh]
                acc_ref[hh] = accs[hh]

            def step(loop):
                it, _ = loop
                j = i - 2 - it
                new = []
                for hh in range(n_heads):
                    lsz, l1m = log_terms(scores(q_ref[0, hh, sub], k_ref[0, hh, j]))
                    after = jnp.dot(l1m.astype(BF16), tri, preferred_element_type=F32)
                    a = jnp.exp2(lsz + after + carry_ref[hh])
                    acc_ref[hh] += jnp.dot(a.astype(BF16), v_ref[0, hh, j],
                                           preferred_element_type=F32)
                    new.append(carry_ref[hh] + after[:, 0:1] + l1m[:, 0:1])
                    carry_ref[hh] = new[-1]
                return it + 1, any_alive(new)

            lax.while_loop(lambda loop: jnp.logical_and(loop[0] < i - 1, loop[1]),
                           step, (jnp.int32(0), True))
            o_ref[0, sub * blk:(sub + 1) * blk, :] = jnp.concatenate(
                [acc_ref[hh] for hh in range(n_heads)], axis=-1).astype(BF16)

    subs = range(n_sub)
    logs = [stage_scores(sub) for sub in subs]
    for sub in subs:
        stage_pool_branch(sub)
    sums = [stage_sums(l1m_all) for _, l1m_all in logs]
    firsts = [stage_values(sub, *logs[sub], *sums[sub]) for sub in subs]
    for sub, (i, carries, accs) in enumerate(firsts):
        earlier_blocks(sub, i, carries, accs)


def _out_kernel(x_ref, osb_ref, gy_ref, gates_ref, w_brs_ref, w_out_ref, g_post_mix_ref,
                g_pre_mlp_ref, w_up_ref, w_down_ref, g_post_mlp_ref, o_ref):
    def mixed(r):
        y_sb = jnp.dot(osb_ref[0, r], w_brs_ref[...], preferred_element_type=F32)
        gate_sb = gates_ref[0, r, y_sb.shape[1]:].astype(F32)
        merged = gy_ref[0, r].astype(F32) + gate_sb * y_sb
        return jnp.dot(merged.astype(BF16), w_out_ref[...], preferred_element_type=F32)

    def normed(r, mix):
        x1 = x_ref[0, r] + mix * _rms_scale(mix) * g_post_mix_ref[...]
        return x1, (x1 * _rms_scale(x1) * g_pre_mlp_ref[...]).astype(BF16)

    def mlp(h2):
        ff = None
        for c0 in range(0, w_up_ref.shape[1], FF_CHUNK):
            up = jnp.dot(h2, w_up_ref[:, c0:c0 + FF_CHUNK], preferred_element_type=F32)
            act = jnp.square(jnp.maximum(up, 0.0)).astype(BF16)
            part = jnp.dot(act, w_down_ref[c0:c0 + FF_CHUNK, :], preferred_element_type=F32)
            ff = part if ff is None else ff + part
        return ff

    half = x_ref.shape[1] // 2
    halves = (pl.ds(0, half), pl.ds(half, half))
    mixes = [mixed(r) for r in halves]
    normeds = [normed(r, mix) for r, mix in zip(halves, mixes)]
    for r, (x1, h2) in zip(halves, normeds):
        ff = mlp(h2)
        o_ref[0, r] = x1 + ff * _rms_scale(ff) * g_post_mlp_ref[...]


def _as_row(a):
    return a.reshape(1, -1).astype(F32)


def _tok_spec(width):
    return pl.BlockSpec((1, TOK_TILE, width), lambda b, s: (b, s, 0))


def _project(x, g_pre_mix, w_in, w_pool_mix, pool_scale, w_gate, b_gate, n_heads,
             later_weights):
    B, S, D = x.shape
    sub = TOK_TILE // ATT_BLK
    n_tiles = S // TOK_TILE
    n_steps = B * n_tiles
    assert all(w.ndim == 2 and w.shape[0] % (BF16_ROWS * n_steps) == 0 for w in later_weights)
    slice_specs = [pl.BlockSpec((w.shape[0] // n_steps, w.shape[1]),
                                lambda b, s: (b * n_tiles + s, 0)) for w in later_weights]
    head_tile = pl.BlockSpec((1, n_heads, sub, ATT_BLK, HEAD_DIM), lambda b, s: (b, 0, s, 0, 0))
    head_shape = jax.ShapeDtypeStruct((B, n_heads, S // ATT_BLK, ATT_BLK, HEAD_DIM), BF16)
    return pl.pallas_call(
        functools.partial(_proj_kernel, n_heads=n_heads, n_cast=len(later_weights)),
        grid=(B, n_tiles),
        in_specs=[
            _tok_spec(D),
            _const_spec((1, D)),
            _const_spec(w_in.shape),
            _const_spec(w_pool_mix.shape),
            _const_spec((1, POOL_WIDTH)),
            _const_spec(w_gate.shape),
            _const_spec((1, 2 * D)),
            *slice_specs,
        ],
        out_specs=[head_tile, head_tile, head_tile, _tok_spec(POOL_WIDTH), _tok_spec(2 * D),
                   *slice_specs],
        out_shape=[head_shape, head_shape, head_shape,
                   jax.ShapeDtypeStruct((B, S, POOL_WIDTH), BF16),
                   jax.ShapeDtypeStruct((B, S, 2 * D), BF16),
                   *[jax.ShapeDtypeStruct(w.shape, BF16) for w in later_weights]],
        scratch_shapes=[pltpu.VMEM((HALO + TOK_TILE, POOL_WIDTH), F32),
                        pltpu.VMEM((len(POOL_WINDOWS), HALO + TOK_TILE, POOL_GROUP), F32)],
        compiler_params=pltpu.CompilerParams(
            dimension_semantics=("arbitrary", "arbitrary"),
            vmem_limit_bytes=VMEM_LIMIT_BYTES),
        name="proj_pool_gates",
    )(x, _as_row(g_pre_mix), w_in.astype(BF16), w_pool_mix.astype(BF16), _as_row(pool_scale),
      w_gate.astype(BF16), _as_row(b_gate), *later_weights)


def _attention(q, k, v, y, gates, w_br_pool):
    B, n_heads, n_blk = q.shape[:3]
    assert n_blk % ATT_QBLKS == 0
    d_model = w_br_pool.shape[1]
    rows_spec = lambda width: pl.BlockSpec((1, ATT_QBLKS * ATT_BLK, width),
                                           lambda b, i: (b, i, 0))
    q_spec = pl.BlockSpec((1, n_heads, ATT_QBLKS, ATT_BLK, HEAD_DIM),
                          lambda b, i: (b, 0, i, 0, 0))
    seq_spec = pl.BlockSpec((1, n_heads, n_blk, ATT_BLK, HEAD_DIM), lambda b, i: (b, 0, 0, 0, 0))
    sb_width = n_heads * HEAD_DIM
    return pl.pallas_call(
        _attn_kernel,
        grid=(B, n_blk // ATT_QBLKS),
        in_specs=[q_spec, seq_spec, seq_spec, rows_spec(y.shape[-1]), rows_spec(gates.shape[-1]),
                  _const_spec(w_br_pool.shape)],
        out_specs=[rows_spec(sb_width), rows_spec(d_model)],
        out_shape=[jax.ShapeDtypeStruct((B, n_blk * ATT_BLK, sb_width), BF16),
                   jax.ShapeDtypeStruct((B, n_blk * ATT_BLK, d_model), BF16)],
        scratch_shapes=[pltpu.VMEM((n_heads, ATT_BLK, 1), F32),
                        pltpu.VMEM((n_heads, ATT_BLK, HEAD_DIM), F32)],
        compiler_params=pltpu.CompilerParams(
            dimension_semantics=("parallel", "parallel"),
            vmem_limit_bytes=VMEM_LIMIT_BYTES),
        name="stickbreak_attn",
    )(q, k, v, y, gates, w_br_pool)


def _merge_mlp(x, o_sb, gy, gates, w_br_sb, w_out, g_post_mix, g_pre_mlp, w_up, w_down,
               g_post_mlp):
    B, S, D = x.shape
    return pl.pallas_call(
        _out_kernel,
        grid=(B, S // TOK_TILE),
        in_specs=[
            _tok_spec(D),
            _tok_spec(o_sb.shape[-1]),
            _tok_spec(D),
            _tok_spec(2 * D),
            _const_spec(w_br_sb.shape),
            _const_spec(w_out.shape),
            _const_spec((1, D)),
            _const_spec((1, D)),
            _const_spec(w_up.shape),
            _const_spec(w_down.shape),
            _const_spec((1, D)),
        ],
        out_specs=_tok_spec(D),
        out_shape=jax.ShapeDtypeStruct((B, S, D), x.dtype),
        compiler_params=pltpu.CompilerParams(
            dimension_semantics=("parallel", "parallel"),
            vmem_limit_bytes=VMEM_LIMIT_BYTES),
        name="merge_mlp",
    )(x, o_sb, gy, gates, w_br_sb, w_out, _as_row(g_post_mix), _as_row(g_pre_mlp), w_up, w_down,
      _as_row(g_post_mlp))


def kernel(x, g_pre_mix, w_in, w_pool_mix, pool_scale, w_br_pool, w_br_sb, w_gate, b_gate,
           w_out, g_post_mix, g_pre_mlp, w_up, w_down, g_post_mlp):
    B, S, D = x.shape
    sb_width = w_br_sb.shape[0]
    assert w_in.shape[1] == POOL_WIDTH + 3 * sb_width and w_br_pool.shape[0] == POOL_WIDTH
    assert S % TOK_TILE == 0 and TOK_TILE % ATT_BLK == 0 and w_gate.shape[1] == 2 * D
    assert w_up.shape[1] % FF_CHUNK == 0

    q, k, v, y, gates, w_br_pool, w_br_sb, w_out, w_up, w_down = _project(
        x, g_pre_mix, w_in, w_pool_mix, pool_scale, w_gate, b_gate,
        sb_width // HEAD_DIM, later_weights=(w_br_pool, w_br_sb, w_out, w_up, w_down))
    o_sb, gy = _attention(q, k, v, y, gates, w_br_pool)
    return _merge_mlp(x, o_sb, gy, gates, w_br_sb, w_out, g_post_mix, g_pre_mlp, w_up, w_down,
                      g_post_mlp)
```

```python
import functools

import jax
import jax.numpy as jnp
from jax import lax
from jax.experimental import pallas as pl
from jax.experimental.pallas import tpu as pltpu

RMS_EPS = 1e-6
POOL_WINDOWS = (2, 4, 8, 16)
POOL_GROUP = 128
POOL_WIDTH = POOL_GROUP * len(POOL_WINDOWS)
HEAD_DIM = 64
SUBLANES = 8
BF16_ROWS = 16
HALO = 4 * SUBLANES
TOK_TILE = 512
ATT_BLK = 256
ATT_QBLKS = 2
FF_CHUNK = 1024
DEAD_LOG2 = -151.0
LOG2E = 1.4426950408889634

VMEM_LIMIT_BYTES = 52 * 1024 * 1024
PROJ_VMEM_LIMIT_BYTES = 57 * 1024 * 1024

BF16 = jnp.bfloat16
F32 = jnp.float32


def _rms_scale(x):
    return lax.rsqrt(jnp.mean(x * x, axis=-1, keepdims=True) + RMS_EPS)


def _const_spec(shape):
    zeros = (0,) * len(shape)
    return pl.BlockSpec(shape, lambda *_: zeros, pipeline_mode=pl.Buffered(1))


def _proj_kernel(x_ref, g_pre_ref, w_in32_ref, w_pm32_ref, pscale_ref, w_brp32_ref,
                 w_gate32_ref, b_gate_ref, *rest, n_heads, d_model, n_cast):
    cast_src, rest = rest[:n_cast], rest[n_cast:]
    q_ref, k_ref, v_ref, gy_ref, gsb_ref = rest[:5]
    cast_dst, rest = rest[5:5 + n_cast], rest[5 + n_cast:]
    uext_ref, lvl_ref, w_in_ref, w_pm_ref, w_brp_ref, w_gate_ref = rest
    s = pl.program_id(1)
    tm = x_ref.shape[1]
    sb_width = n_heads * HEAD_DIM

    @pl.when(jnp.logical_and(pl.program_id(0) == 0, s == 0))
    def _():
        uext_ref[tm:tm + HALO, :] = jnp.zeros((HALO, POOL_WIDTH), F32)
        for src, dst in ((w_in32_ref, w_in_ref), (w_pm32_ref, w_pm_ref),
                         (w_brp32_ref, w_brp_ref), (w_gate32_ref, w_gate_ref)):
            dst[...] = src[...].astype(BF16)

    hs = []
    for r in (pl.ds(0, tm // 2), pl.ds(tm // 2, tm // 2)):
        x = x_ref[0, r]
        hs.append((x * _rms_scale(x) * g_pre_ref[...]).astype(BF16))
    both = lambda w_ref: jnp.concatenate(
        [jnp.dot(h, w_ref[...], preferred_element_type=F32) for h in hs], axis=0)
    proj = both(w_in_ref)
    gates = jax.nn.sigmoid(both(w_gate_ref) + b_gate_ref[...])
    gsb_ref[0] = gates[:, d_model:].astype(BF16)

    q = proj[:, POOL_WIDTH:POOL_WIDTH + sb_width] * (HEAD_DIM ** -0.5 * LOG2E)
    k = proj[:, POOL_WIDTH + sb_width:POOL_WIDTH + 2 * sb_width]
    v = proj[:, POOL_WIDTH + 2 * sb_width:POOL_WIDTH + 3 * sb_width]
    for src, dst in ((q, q_ref), (k, k_ref), (v, v_ref)):
        for hh in range(n_heads):
            cols = src[:, hh * HEAD_DIM:(hh + 1) * HEAD_DIM].astype(BF16)
            for t in range(tm // ATT_BLK):
                dst[0, hh, t] = cols[t * ATT_BLK:(t + 1) * ATT_BLK]

    uext_ref[0:HALO, :] = jnp.where(s > 0, uext_ref[tm:tm + HALO, :], 0.0)

    u = proj[:, :POOL_WIDTH]
    uext_ref[HALO:HALO + tm, :] = u

    for src, dst in zip(cast_src, cast_dst):
        dst[...] = src[...].astype(BF16)

    pos = s * tm + lax.broadcasted_iota(jnp.int32, (tm, POOL_GROUP), 0)
    rows = HALO + tm
    mixed = []
    for g, w in enumerate(POOL_WINDOWS):
        c0, c1 = g * POOL_GROUP, (g + 1) * POOL_GROUP
        levels = w.bit_length() - 1
        start = HALO - SUBLANES * (levels - 1)
        win = uext_ref[start:rows, c0:c1] + uext_ref[start - 1:rows - 1, c0:c1]
        for lvl in range(2, levels + 1):
            back = 1 << (lvl - 1)
            lvl_ref[g, start:rows, :] = win
            start += SUBLANES
            win = win[SUBLANES:] + lvl_ref[g, start - back:rows - back, :]
        count = jnp.minimum(pos + 1, w).astype(F32)
        pooled = win / count - u[:, c0:c1]
        mixed.append(jnp.dot(pooled.astype(BF16), w_pm_ref[g], preferred_element_type=F32))
    y = jnp.concatenate(mixed, axis=-1) * pscale_ref[...]
    y_pool = jnp.dot(y.astype(BF16), w_brp_ref[...], preferred_element_type=F32)

    gy_ref[0] = (gates[:, :d_model] * y_pool).astype(BF16)


def _attn_kernel(q_ref, k_ref, v_ref, o_ref, carry_ref, acc_ref):
    n_heads, n_sub, blk, _ = q_ref.shape[1:]
    wide = 2 * blk

    def strict_lower(n):
        r = lax.broadcasted_iota(jnp.int32, (n, n), 0)
        c = lax.broadcasted_iota(jnp.int32, (n, n), 1)
        return jnp.where(r > c, 1.0, 0.0).astype(BF16)

    def scores(q, keys):
        return lax.dot_general(q, keys, (((1,), (1,)), ((), ())), preferred_element_type=F32)

    def log_terms(z):
        soft = jnp.log(1.0 + jnp.exp2(-jnp.abs(z))) * LOG2E
        lsz = jnp.minimum(z, 0.0) - soft
        return lsz, lsz - z

    tri = strict_lower(blk)
    row = lax.broadcasted_iota(jnp.int32, (blk, wide), 0)
    col = lax.broadcasted_iota(jnp.int32, (blk, wide), 1)
    causal = col < row + blk

    def first_tile(sub):
        i = pl.program_id(1) * n_sub + sub
        prev = jnp.maximum(i - 1, 0)
        lsz_all, l1m_all = [], []
        for hh in range(n_heads):
            keys = jnp.concatenate([k_ref[0, hh, prev], k_ref[0, hh, i]], axis=0)
            lsz, l1m = log_terms(scores(q_ref[0, hh, sub], keys))
            lsz_all.append(lsz)
            l1m_all.append(jnp.where(causal, l1m, 0.0))
        stack = lambda lo: jnp.concatenate([l1m[:, lo:lo + blk] for l1m in l1m_all],
                                           axis=0).astype(BF16)
        after_prev_all = jnp.dot(stack(0), tri, preferred_element_type=F32)
        after_diag_all = jnp.dot(stack(blk), tri, preferred_element_type=F32)
        carries, accs = [], []
        for hh in range(n_heads):
            l1m = l1m_all[hh]
            after_prev = after_prev_all[hh * blk:(hh + 1) * blk]
            after_diag = after_diag_all[hh * blk:(hh + 1) * blk]
            total_diag = after_diag[:, 0:1] + l1m[:, blk:blk + 1]
            after = jnp.concatenate([after_prev + total_diag, after_diag], axis=1)
            a = jnp.where(causal, jnp.exp2(lsz_all[hh] + after), 0.0)
            v_prev = v_ref[0, hh, prev]
            if sub == 0:
                v_prev = jnp.where(i > 0, v_prev, jnp.zeros((), BF16))
            vals = jnp.concatenate([v_prev, v_ref[0, hh, i]], axis=0)
            accs.append(jnp.dot(a.astype(BF16), vals, preferred_element_type=F32))
            carries.append(after_prev[:, 0:1] + total_diag + l1m[:, 0:1])
        o_ref[0, sub * blk:(sub + 1) * blk, :] = jnp.concatenate(accs, axis=-1).astype(BF16)
        return i, carries, accs

    def any_alive(carries):
        return jnp.max(functools.reduce(jnp.maximum, carries)) > DEAD_LOG2

    def earlier_blocks(sub, i, carries, accs):
        @pl.when(jnp.logical_and(i > 1, any_alive(carries)))
        def _():
            for hh in range(n_heads):
                carry_ref[hh] = carries[hh]
                acc_ref[hh] = accs[hh]

            def step(loop):
                it, _ = loop
                j = i - 2 - it
                new = []
                for hh in range(n_heads):
                    lsz, l1m = log_terms(scores(q_ref[0, hh, sub], k_ref[0, hh, j]))
                    after = jnp.dot(l1m.astype(BF16), tri, preferred_element_type=F32)
                    a = jnp.exp2(lsz + after + carry_ref[hh])
                    acc_ref[hh] += jnp.dot(a.astype(BF16), v_ref[0, hh, j],
                                           preferred_element_type=F32)
                    new.append(carry_ref[hh] + after[:, 0:1] + l1m[:, 0:1])
                    carry_ref[hh] = new[-1]
                return it + 1, any_alive(new)

            lax.while_loop(lambda loop: jnp.logical_and(loop[0] < i - 1, loop[1]),
                           step, (jnp.int32(0), True))
            o_ref[0, sub * blk:(sub + 1) * blk, :] = jnp.concatenate(
                [acc_ref[hh] for hh in range(n_heads)], axis=-1).astype(BF16)

    firsts = [first_tile(sub) for sub in range(n_sub)]
    for sub, (i, carries, accs) in enumerate(firsts):
        earlier_blocks(sub, i, carries, accs)


def _out_kernel(x_ref, osb_ref, gy_ref, gsb_ref, w_brs_ref, w_out_ref, g_post_mix_ref,
                g_pre_mlp_ref, w_up_ref, w_down_ref, g_post_mlp_ref, o_ref):
    def mixed(r):
        y_sb = jnp.dot(osb_ref[0, r], w_brs_ref[...], preferred_element_type=F32)
        merged = gy_ref[0, r].astype(F32) + gsb_ref[0, r].astype(F32) * y_sb
        return jnp.dot(merged.astype(BF16), w_out_ref[...], preferred_element_type=F32)

    def normed(r, mix):
        x1 = x_ref[0, r] + mix * _rms_scale(mix) * g_post_mix_ref[...]
        return x1, (x1 * _rms_scale(x1) * g_pre_mlp_ref[...]).astype(BF16)

    def mlp(h2):
        ff = None
        for c0 in range(0, w_up_ref.shape[1], FF_CHUNK):
            up = jnp.dot(h2, w_up_ref[:, c0:c0 + FF_CHUNK], preferred_element_type=F32)
            act = jnp.square(jnp.maximum(up, 0.0)).astype(BF16)
            part = jnp.dot(act, w_down_ref[c0:c0 + FF_CHUNK, :], preferred_element_type=F32)
            ff = part if ff is None else ff + part
        return ff

    half = x_ref.shape[1] // 2
    halves = (pl.ds(0, half), pl.ds(half, half))
    mixes = [mixed(r) for r in halves]
    normeds = [normed(r, mix) for r, mix in zip(halves, mixes)]
    for r, (x1, h2) in zip(halves, normeds):
        ff = mlp(h2)
        o_ref[0, r] = x1 + ff * _rms_scale(ff) * g_post_mlp_ref[...]


def _as_row(a):
    return a.reshape(1, -1).astype(F32)


def _tok_spec(width):
    return pl.BlockSpec((1, TOK_TILE, width), lambda b, s: (b, s, 0))


def _project(x, g_pre_mix, w_in, w_pool_mix, pool_scale, w_br_pool, w_gate, b_gate, n_heads,
             later_weights):
    B, S, D = x.shape
    sub = TOK_TILE // ATT_BLK
    n_tiles = S // TOK_TILE
    n_steps = B * n_tiles
    assert all(w.ndim == 2 and w.shape[0] % (BF16_ROWS * n_steps) == 0 for w in later_weights)
    slice_specs = [pl.BlockSpec((w.shape[0] // n_steps, w.shape[1]),
                                lambda b, s: (b * n_tiles + s, 0)) for w in later_weights]
    head_tile = pl.BlockSpec((1, n_heads, sub, ATT_BLK, HEAD_DIM), lambda b, s: (b, 0, s, 0, 0))
    head_shape = jax.ShapeDtypeStruct((B, n_heads, S // ATT_BLK, ATT_BLK, HEAD_DIM), BF16)
    return pl.pallas_call(
        functools.partial(_proj_kernel, n_heads=n_heads, d_model=D,
                          n_cast=len(later_weights)),
        grid=(B, n_tiles),
        in_specs=[
            _tok_spec(D),
            _const_spec((1, D)),
            _const_spec(w_in.shape),
            _const_spec(w_pool_mix.shape),
            _const_spec((1, POOL_WIDTH)),
            _const_spec(w_br_pool.shape),
            _const_spec(w_gate.shape),
            _const_spec((1, 2 * D)),
            *slice_specs,
        ],
        out_specs=[head_tile, head_tile, head_tile, _tok_spec(D), _tok_spec(D), *slice_specs],
        out_shape=[head_shape, head_shape, head_shape,
                   jax.ShapeDtypeStruct((B, S, D), BF16),
                   jax.ShapeDtypeStruct((B, S, D), BF16),
                   *[jax.ShapeDtypeStruct(w.shape, BF16) for w in later_weights]],
        scratch_shapes=[pltpu.VMEM((HALO + TOK_TILE, POOL_WIDTH), F32),
                        pltpu.VMEM((len(POOL_WINDOWS), HALO + TOK_TILE, POOL_GROUP), F32),
                        pltpu.VMEM(w_in.shape, BF16), pltpu.VMEM(w_pool_mix.shape, BF16),
                        pltpu.VMEM(w_br_pool.shape, BF16), pltpu.VMEM(w_gate.shape, BF16)],
        compiler_params=pltpu.CompilerParams(
            dimension_semantics=("arbitrary", "arbitrary"),
            vmem_limit_bytes=PROJ_VMEM_LIMIT_BYTES),
        name="proj_pool_gates",
    )(x, _as_row(g_pre_mix), w_in, w_pool_mix, _as_row(pool_scale), w_br_pool, w_gate,
      _as_row(b_gate), *later_weights)


def _attention(q, k, v):
    B, n_heads, n_blk = q.shape[:3]
    assert n_blk % ATT_QBLKS == 0
    q_spec = pl.BlockSpec((1, n_heads, ATT_QBLKS, ATT_BLK, HEAD_DIM),
                          lambda b, i: (b, 0, i, 0, 0))
    seq_spec = pl.BlockSpec((1, n_heads, n_blk, ATT_BLK, HEAD_DIM), lambda b, i: (b, 0, 0, 0, 0))
    sb_width = n_heads * HEAD_DIM
    return pl.pallas_call(
        _attn_kernel,
        grid=(B, n_blk // ATT_QBLKS),
        in_specs=[q_spec, seq_spec, seq_spec],
        out_specs=pl.BlockSpec((1, ATT_QBLKS * ATT_BLK, sb_width), lambda b, i: (b, i, 0)),
        out_shape=jax.ShapeDtypeStruct((B, n_blk * ATT_BLK, sb_width), BF16),
        scratch_shapes=[pltpu.VMEM((n_heads, ATT_BLK, 1), F32),
                        pltpu.VMEM((n_heads, ATT_BLK, HEAD_DIM), F32)],
        compiler_params=pltpu.CompilerParams(
            dimension_semantics=("parallel", "parallel"),
            vmem_limit_bytes=VMEM_LIMIT_BYTES),
        name="stickbreak_attn",
    )(q, k, v)


def _merge_mlp(x, o_sb, gy, gsb, w_br_sb, w_out, g_post_mix, g_pre_mlp, w_up, w_down,
               g_post_mlp):
    B, S, D = x.shape
    return pl.pallas_call(
        _out_kernel,
        grid=(B, S // TOK_TILE),
        in_specs=[
            _tok_spec(D),
            _tok_spec(o_sb.shape[-1]),
            _tok_spec(D),
            _tok_spec(D),
            _const_spec(w_br_sb.shape),
            _const_spec(w_out.shape),
            _const_spec((1, D)),
            _const_spec((1, D)),
            _const_spec(w_up.shape),
            _const_spec(w_down.shape),
            _const_spec((1, D)),
        ],
        out_specs=_tok_spec(D),
        out_shape=jax.ShapeDtypeStruct((B, S, D), x.dtype),
        compiler_params=pltpu.CompilerParams(
            dimension_semantics=("parallel", "parallel"),
            vmem_limit_bytes=VMEM_LIMIT_BYTES),
        name="merge_mlp",
    )(x, o_sb, gy, gsb, w_br_sb, w_out, _as_row(g_post_mix), _as_row(g_pre_mlp), w_up, w_down,
      _as_row(g_post_mlp))


def kernel(x, g_pre_mix, w_in, w_pool_mix, pool_scale, w_br_pool, w_br_sb, w_gate, b_gate,
           w_out, g_post_mix, g_pre_mlp, w_up, w_down, g_post_mlp):
    B, S, D = x.shape
    sb_width = w_br_sb.shape[0]
    assert w_in.shape[1] == POOL_WIDTH + 3 * sb_width and w_br_pool.shape[0] == POOL_WIDTH
    assert S % TOK_TILE == 0 and TOK_TILE % ATT_BLK == 0 and w_gate.shape[1] == 2 * D
    assert w_up.shape[1] % FF_CHUNK == 0

    q, k, v, gy, gsb, w_br_sb, w_out, w_up, w_down = _project(
        x, g_pre_mix, w_in, w_pool_mix, pool_scale, w_br_pool, w_gate, b_gate,
        sb_width // HEAD_DIM, later_weights=(w_br_sb, w_out, w_up, w_down))
    o_sb = _attention(q, k, v)
    return _merge_mlp(x, o_sb, gy, gsb, w_br_sb, w_out, g_post_mix, g_pre_mlp, w_up, w_down,
                      g_post_mlp)
```

```python
import functools

import jax
import jax.numpy as jnp
from jax import lax
from jax.experimental import pallas as pl
from jax.experimental.pallas import tpu as pltpu

RMS_EPS = 1e-6
POOL_WINDOWS = (2, 4, 8, 16)
POOL_GROUP = 128
POOL_WIDTH = POOL_GROUP * len(POOL_WINDOWS)
HEAD_DIM = 64
SUBLANES = 8
BF16_ROWS = 16
HALO = 4 * SUBLANES
TOK_TILE = 512
ATT_BLK = 256
ATT_QBLKS = 2
FF_CHUNK = 1024
DEAD_LOG2 = -151.0
LOG2E = 1.4426950408889634

VMEM_LIMIT_BYTES = 52 * 1024 * 1024
PROJ_VMEM_LIMIT_BYTES = 57 * 1024 * 1024

BF16 = jnp.bfloat16
F32 = jnp.float32


def _rms_scale(x):
    return lax.rsqrt(jnp.mean(x * x, axis=-1, keepdims=True) + RMS_EPS)


def _const_spec(shape):
    zeros = (0,) * len(shape)
    return pl.BlockSpec(shape, lambda *_: zeros, pipeline_mode=pl.Buffered(1))


def _proj_kernel(x_ref, g_pre_ref, w_in32_ref, w_pm32_ref, pscale_ref, w_brp32_ref,
                 w_gate32_ref, b_gate_ref, *rest, n_heads, d_model, n_cast):
    cast_src, rest = rest[:n_cast], rest[n_cast:]
    q_ref, k_ref, v_ref, gy_ref, gsb_ref = rest[:5]
    cast_dst, rest = rest[5:5 + n_cast], rest[5 + n_cast:]
    uext_ref, lvl_ref, w_in_ref, w_pm_ref, w_brp_ref, w_gate_ref = rest
    s = pl.program_id(1)
    tm = x_ref.shape[1]
    sb_width = n_heads * HEAD_DIM

    @pl.when(jnp.logical_and(pl.program_id(0) == 0, s == 0))
    def _():
        uext_ref[tm:tm + HALO, :] = jnp.zeros((HALO, POOL_WIDTH), F32)
        for src, dst in ((w_in32_ref, w_in_ref), (w_pm32_ref, w_pm_ref),
                         (w_brp32_ref, w_brp_ref), (w_gate32_ref, w_gate_ref)):
            dst[...] = src[...].astype(BF16)

    hs = []
    for r in (pl.ds(0, tm // 2), pl.ds(tm // 2, tm // 2)):
        x = x_ref[0, r]
        hs.append((x * _rms_scale(x) * g_pre_ref[...]).astype(BF16))
    both = lambda w_ref: jnp.concatenate(
        [jnp.dot(h, w_ref[...], preferred_element_type=F32) for h in hs], axis=0)
    proj = both(w_in_ref)
    gates = jax.nn.sigmoid(both(w_gate_ref) + b_gate_ref[...])
    gsb_ref[0] = gates[:, d_model:].astype(BF16)

    q = proj[:, POOL_WIDTH:POOL_WIDTH + sb_width] * (HEAD_DIM ** -0.5 * LOG2E)
    k = proj[:, POOL_WIDTH + sb_width:POOL_WIDTH + 2 * sb_width]
    v = proj[:, POOL_WIDTH + 2 * sb_width:POOL_WIDTH + 3 * sb_width]
    for src, dst in ((q, q_ref), (k, k_ref), (v, v_ref)):
        for hh in range(n_heads):
            cols = src[:, hh * HEAD_DIM:(hh + 1) * HEAD_DIM].astype(BF16)
            for t in range(tm // ATT_BLK):
                dst[0, hh, t] = cols[t * ATT_BLK:(t + 1) * ATT_BLK]

    uext_ref[0:HALO, :] = jnp.where(s > 0, uext_ref[tm:tm + HALO, :], 0.0)

    u = proj[:, :POOL_WIDTH]
    uext_ref[HALO:HALO + tm, :] = u

    for src, dst in zip(cast_src, cast_dst):
        dst[...] = src[...].astype(BF16)

    pos = s * tm + lax.broadcasted_iota(jnp.int32, (tm, POOL_GROUP), 0)
    rows = HALO + tm
    mixed = []
    for g, w in enumerate(POOL_WINDOWS):
        c0, c1 = g * POOL_GROUP, (g + 1) * POOL_GROUP
        levels = w.bit_length() - 1
        start = HALO - SUBLANES * (levels - 1)
        win = uext_ref[start:rows, c0:c1] + uext_ref[start - 1:rows - 1, c0:c1]
        for lvl in range(2, levels + 1):
            back = 1 << (lvl - 1)
            lvl_ref[g, start:rows, :] = win
            start += SUBLANES
            win = win[SUBLANES:] + lvl_ref[g, start - back:rows - back, :]
        count = jnp.minimum(pos + 1, w).astype(F32)
        pooled = win / count - u[:, c0:c1]
        mixed.append(jnp.dot(pooled.astype(BF16), w_pm_ref[g], preferred_element_type=F32))
    y = jnp.concatenate(mixed, axis=-1) * pscale_ref[...]
    y_pool = jnp.dot(y.astype(BF16), w_brp_ref[...], preferred_element_type=F32)

    gy_ref[0] = (gates[:, :d_model] * y_pool).astype(BF16)


def _attn_kernel(q_ref, k_ref, v_ref, o_ref, carry_ref, acc_ref):
    n_heads, n_sub, blk, _ = q_ref.shape[1:]
    wide = 2 * blk

    def strict_lower(n):
        r = lax.broadcasted_iota(jnp.int32, (n, n), 0)
        c = lax.broadcasted_iota(jnp.int32, (n, n), 1)
        return jnp.where(r > c, 1.0, 0.0).astype(BF16)

    def scores(q, keys):
        return lax.dot_general(q, keys, (((1,), (1,)), ((), ())), preferred_element_type=F32)

    def log_terms(z):
        soft = jnp.log(1.0 + jnp.exp2(-jnp.abs(z))) * LOG2E
        lsz = jnp.minimum(z, 0.0) - soft
        return lsz, lsz - z

    tri = strict_lower(blk)
    row = lax.broadcasted_iota(jnp.int32, (blk, wide), 0)
    col = lax.broadcasted_iota(jnp.int32, (blk, wide), 1)
    causal = col < row + blk

    def block_ids(sub):
        i = pl.program_id(1) * n_sub + sub
        return i, jnp.maximum(i - 1, 0)

    def stage_scores(sub):
        i, prev = block_ids(sub)
        lsz_all, l1m_all = [], []
        for hh in range(n_heads):
            keys = jnp.concatenate([k_ref[0, hh, prev], k_ref[0, hh, i]], axis=0)
            lsz, l1m = log_terms(scores(q_ref[0, hh, sub], keys))
            lsz_all.append(lsz)
            l1m_all.append(jnp.where(causal, l1m, 0.0))
        return lsz_all, l1m_all

    def stage_values(sub, lsz_all, l1m_all):
        i, prev = block_ids(sub)
        stack = lambda lo: jnp.concatenate([l1m[:, lo:lo + blk] for l1m in l1m_all],
                                           axis=0).astype(BF16)
        after_prev_all = jnp.dot(stack(0), tri, preferred_element_type=F32)
        after_diag_all = jnp.dot(stack(blk), tri, preferred_element_type=F32)
        carries, accs = [], []
        for hh in range(n_heads):
            l1m = l1m_all[hh]
            after_prev = after_prev_all[hh * blk:(hh + 1) * blk]
            after_diag = after_diag_all[hh * blk:(hh + 1) * blk]
            total_diag = after_diag[:, 0:1] + l1m[:, blk:blk + 1]
            after = jnp.concatenate([after_prev + total_diag, after_diag], axis=1)
            a = jnp.where(causal, jnp.exp2(lsz_all[hh] + after), 0.0)
            v_prev = v_ref[0, hh, prev]
            if sub == 0:
                v_prev = jnp.where(i > 0, v_prev, jnp.zeros((), BF16))
            vals = jnp.concatenate([v_prev, v_ref[0, hh, i]], axis=0)
            accs.append(jnp.dot(a.astype(BF16), vals, preferred_element_type=F32))
            carries.append(after_prev[:, 0:1] + total_diag + l1m[:, 0:1])
        o_ref[0, sub * blk:(sub + 1) * blk, :] = jnp.concatenate(accs, axis=-1).astype(BF16)
        return i, carries, accs

    def any_alive(carries):
        return jnp.max(functools.reduce(jnp.maximum, carries)) > DEAD_LOG2

    def earlier_blocks(sub, i, carries, accs):
        @pl.when(jnp.logical_and(i > 1, any_alive(carries)))
        def _():
            for hh in range(n_heads):
                carry_ref[hh] = carries[hh]
                acc_ref[hh] = accs[hh]

            def step(loop):
                it, _ = loop
                j = i - 2 - it
                new = []
                for hh in range(n_heads):
                    lsz, l1m = log_terms(scores(q_ref[0, hh, sub], k_ref[0, hh, j]))
                    after = jnp.dot(l1m.astype(BF16), tri, preferred_element_type=F32)
                    a = jnp.exp2(lsz + after + carry_ref[hh])
                    acc_ref[hh] += jnp.dot(a.astype(BF16), v_ref[0, hh, j],
                                           preferred_element_type=F32)
                    new.append(carry_ref[hh] + after[:, 0:1] + l1m[:, 0:1])
                    carry_ref[hh] = new[-1]
                return it + 1, any_alive(new)

            lax.while_loop(lambda loop: jnp.logical_and(loop[0] < i - 1, loop[1]),
                           step, (jnp.int32(0), True))
            o_ref[0, sub * blk:(sub + 1) * blk, :] = jnp.concatenate(
                [acc_ref[hh] for hh in range(n_heads)], axis=-1).astype(BF16)

    logs = [stage_scores(sub) for sub in range(n_sub)]
    firsts = [stage_values(sub, *logs[sub]) for sub in range(n_sub)]
    for sub, (i, carries, accs) in enumerate(firsts):
        earlier_blocks(sub, i, carries, accs)


def _out_kernel(x_ref, osb_ref, gy_ref, gsb_ref, w_brs_ref, w_out_ref, g_post_mix_ref,
                g_pre_mlp_ref, w_up_ref, w_down_ref, g_post_mlp_ref, o_ref):
    def mixed(r):
        y_sb = jnp.dot(osb_ref[0, r], w_brs_ref[...], preferred_element_type=F32)
        merged = gy_ref[0, r].astype(F32) + gsb_ref[0, r].astype(F32) * y_sb
        return jnp.dot(merged.astype(BF16), w_out_ref[...], preferred_element_type=F32)

    def normed(r, mix):
        x1 = x_ref[0, r] + mix * _rms_scale(mix) * g_post_mix_ref[...]
        return x1, (x1 * _rms_scale(x1) * g_pre_mlp_ref[...]).astype(BF16)

    def mlp(h2):
        ff = None
        for c0 in range(0, w_up_ref.shape[1], FF_CHUNK):
            up = jnp.dot(h2, w_up_ref[:, c0:c0 + FF_CHUNK], preferred_element_type=F32)
            act = jnp.square(jnp.maximum(up, 0.0)).astype(BF16)
            part = jnp.dot(act, w_down_ref[c0:c0 + FF_CHUNK, :], preferred_element_type=F32)
            ff = part if ff is None else ff + part
        return ff

    half = x_ref.shape[1] // 2
    halves = (pl.ds(0, half), pl.ds(half, half))
    mixes = [mixed(r) for r in halves]
    normeds = [normed(r, mix) for r, mix in zip(halves, mixes)]
    for r, (x1, h2) in zip(halves, normeds):
        ff = mlp(h2)
        o_ref[0, r] = x1 + ff * _rms_scale(ff) * g_post_mlp_ref[...]


def _as_row(a):
    return a.reshape(1, -1).astype(F32)


def _tok_spec(width):
    return pl.BlockSpec((1, TOK_TILE, width), lambda b, s: (b, s, 0))


def _project(x, g_pre_mix, w_in, w_pool_mix, pool_scale, w_br_pool, w_gate, b_gate, n_heads,
             later_weights):
    B, S, D = x.shape
    sub = TOK_TILE // ATT_BLK
    n_tiles = S // TOK_TILE
    n_steps = B * n_tiles
    assert all(w.ndim == 2 and w.shape[0] % (BF16_ROWS * n_steps) == 0 for w in later_weights)
    slice_specs = [pl.BlockSpec((w.shape[0] // n_steps, w.shape[1]),
                                lambda b, s: (b * n_tiles + s, 0)) for w in later_weights]
    head_tile = pl.BlockSpec((1, n_heads, sub, ATT_BLK, HEAD_DIM), lambda b, s: (b, 0, s, 0, 0))
    head_shape = jax.ShapeDtypeStruct((B, n_heads, S // ATT_BLK, ATT_BLK, HEAD_DIM), BF16)
    return pl.pallas_call(
        functools.partial(_proj_kernel, n_heads=n_heads, d_model=D,
                          n_cast=len(later_weights)),
        grid=(B, n_tiles),
        in_specs=[
            _tok_spec(D),
            _const_spec((1, D)),
            _const_spec(w_in.shape),
            _const_spec(w_pool_mix.shape),
            _const_spec((1, POOL_WIDTH)),
            _const_spec(w_br_pool.shape),
            _const_spec(w_gate.shape),
            _const_spec((1, 2 * D)),
            *slice_specs,
        ],
        out_specs=[head_tile, head_tile, head_tile, _tok_spec(D), _tok_spec(D), *slice_specs],
        out_shape=[head_shape, head_shape, head_shape,
                   jax.ShapeDtypeStruct((B, S, D), BF16),
                   jax.ShapeDtypeStruct((B, S, D), BF16),
                   *[jax.ShapeDtypeStruct(w.shape, BF16) for w in later_weights]],
        scratch_shapes=[pltpu.VMEM((HALO + TOK_TILE, POOL_WIDTH), F32),
                        pltpu.VMEM((len(POOL_WINDOWS), HALO + TOK_TILE, POOL_GROUP), F32),
                        pltpu.VMEM(w_in.shape, BF16), pltpu.VMEM(w_pool_mix.shape, BF16),
                        pltpu.VMEM(w_br_pool.shape, BF16), pltpu.VMEM(w_gate.shape, BF16)],
        compiler_params=pltpu.CompilerParams(
            dimension_semantics=("arbitrary", "arbitrary"),
            vmem_limit_bytes=PROJ_VMEM_LIMIT_BYTES),
        name="proj_pool_gates",
    )(x, _as_row(g_pre_mix), w_in, w_pool_mix, _as_row(pool_scale), w_br_pool, w_gate,
      _as_row(b_gate), *later_weights)


def _attention(q, k, v):
    B, n_heads, n_blk = q.shape[:3]
    assert n_blk % ATT_QBLKS == 0
    q_spec = pl.BlockSpec((1, n_heads, ATT_QBLKS, ATT_BLK, HEAD_DIM),
                          lambda b, i: (b, 0, i, 0, 0))
    seq_spec = pl.BlockSpec((1, n_heads, n_blk, ATT_BLK, HEAD_DIM), lambda b, i: (b, 0, 0, 0, 0))
    sb_width = n_heads * HEAD_DIM
    return pl.pallas_call(
        _attn_kernel,
        grid=(B, n_blk // ATT_QBLKS),
        in_specs=[q_spec, seq_spec, seq_spec],
        out_specs=pl.BlockSpec((1, ATT_QBLKS * ATT_BLK, sb_width), lambda b, i: (b, i, 0)),
        out_shape=jax.ShapeDtypeStruct((B, n_blk * ATT_BLK, sb_width), BF16),
        scratch_shapes=[pltpu.VMEM((n_heads, ATT_BLK, 1), F32),
                        pltpu.VMEM((n_heads, ATT_BLK, HEAD_DIM), F32)],
        compiler_params=pltpu.CompilerParams(
            dimension_semantics=("parallel", "parallel"),
            vmem_limit_bytes=VMEM_LIMIT_BYTES),
        name="stickbreak_attn",
    )(q, k, v)


def _merge_mlp(x, o_sb, gy, gsb, w_br_sb, w_out, g_post_mix, g_pre_mlp, w_up, w_down,
               g_post_mlp):
    B, S, D = x.shape
    return pl.pallas_call(
        _out_kernel,
        grid=(B, S // TOK_TILE),
        in_specs=[
            _tok_spec(D),
            _tok_spec(o_sb.shape[-1]),
            _tok_spec(D),
            _tok_spec(D),
            _const_spec(w_br_sb.shape),
            _const_spec(w_out.shape),
            _const_spec((1, D)),
            _const_spec((1, D)),
            _const_spec(w_up.shape),
            _const_spec(w_down.shape),
            _const_spec((1, D)),
        ],
        out_specs=_tok_spec(D),
        out_shape=jax.ShapeDtypeStruct((B, S, D), x.dtype),
        compiler_params=pltpu.CompilerParams(
            dimension_semantics=("parallel", "parallel"),
            vmem_limit_bytes=VMEM_LIMIT_BYTES),
        name="merge_mlp",
    )(x, o_sb, gy, gsb, w_br_sb, w_out, _as_row(g_post_mix), _as_row(g_pre_mlp), w_up, w_down,
      _as_row(g_post_mlp))


def kernel(x, g_pre_mix, w_in, w_pool_mix, pool_scale, w_br_pool, w_br_sb, w_gate, b_gate,
           w_out, g_post_mix, g_pre_mlp, w_up, w_down, g_post_mlp):
    B, S, D = x.shape
    sb_width = w_br_sb.shape[0]
    assert w_in.shape[1] == POOL_WIDTH + 3 * sb_width and w_br_pool.shape[0] == POOL_WIDTH
    assert S % TOK_TILE == 0 and TOK_TILE % ATT_BLK == 0 and w_gate.shape[1] == 2 * D
    assert w_up.shape[1] % FF_CHUNK == 0

    q, k, v, gy, gsb, w_br_sb, w_out, w_up, w_down = _project(
        x, g_pre_mix, w_in, w_pool_mix, pool_scale, w_br_pool, w_gate, b_gate,
        sb_width // HEAD_DIM, later_weights=(w_br_sb, w_out, w_up, w_down))
    o_sb = _attention(q, k, v)
    return _merge_mlp(x, o_sb, gy, gsb, w_br_sb, w_out, g_post_mix, g_pre_mlp, w_up, w_down,
                      g_post_mlp)
```

```python
import functools

import jax
import jax.numpy as jnp
from jax import lax
from jax.experimental import pallas as pl
from jax.experimental.pallas import tpu as pltpu

RMS_EPS = 1e-6
POOL_WINDOWS = (2, 4, 8, 16)
POOL_GROUP = 128
POOL_WIDTH = POOL_GROUP * len(POOL_WINDOWS)
HEAD_DIM = 64
SUBLANES = 8
BF16_ROWS = 16
HALO = 4 * SUBLANES
TOK_TILE = 512
ATT_BLK = 256
ATT_QBLKS = 2
FF_CHUNK = 1024
DEAD_LOG2 = -151.0
LOG2E = 1.4426950408889634

VMEM_LIMIT_BYTES = 52 * 1024 * 1024
PROJ_VMEM_LIMIT_BYTES = 57 * 1024 * 1024

BF16 = jnp.bfloat16
F32 = jnp.float32


def _rms_scale(x):
    return lax.rsqrt(jnp.mean(x * x, axis=-1, keepdims=True) + RMS_EPS)


def _const_spec(shape):
    zeros = (0,) * len(shape)
    return pl.BlockSpec(shape, lambda *_: zeros, pipeline_mode=pl.Buffered(1))


def _proj_kernel(x_ref, g_pre_ref, w_in32_ref, w_pm32_ref, pscale_ref, w_brp32_ref,
                 w_gate32_ref, b_gate_ref, *rest, n_heads, d_model, n_cast):
    cast_src, rest = rest[:n_cast], rest[n_cast:]
    q_ref, k_ref, v_ref, gy_ref, gsb_ref = rest[:5]
    cast_dst, rest = rest[5:5 + n_cast], rest[5 + n_cast:]
    uext_ref, lvl_ref, w_in_ref, w_pm_ref, w_brp_ref, w_gate_ref = rest
    s = pl.program_id(1)
    tm = x_ref.shape[1]
    sb_width = n_heads * HEAD_DIM

    @pl.when(jnp.logical_and(pl.program_id(0) == 0, s == 0))
    def _():
        uext_ref[tm:tm + HALO, :] = jnp.zeros((HALO, POOL_WIDTH), F32)
        for src, dst in ((w_in32_ref, w_in_ref), (w_pm32_ref, w_pm_ref),
                         (w_brp32_ref, w_brp_ref), (w_gate32_ref, w_gate_ref)):
            dst[...] = src[...].astype(BF16)

    hs = []
    for r in (pl.ds(0, tm // 2), pl.ds(tm // 2, tm // 2)):
        x = x_ref[0, r]
        hs.append((x * _rms_scale(x) * g_pre_ref[...]).astype(BF16))
    both = lambda w_ref: jnp.concatenate(
        [jnp.dot(h, w_ref[...], preferred_element_type=F32) for h in hs], axis=0)
    proj = both(w_in_ref)
    gates = jax.nn.sigmoid(both(w_gate_ref) + b_gate_ref[...])
    gsb_ref[0] = gates[:, d_model:].astype(BF16)

    q = proj[:, POOL_WIDTH:POOL_WIDTH + sb_width] * (HEAD_DIM ** -0.5 * LOG2E)
    k = proj[:, POOL_WIDTH + sb_width:POOL_WIDTH + 2 * sb_width]
    v = proj[:, POOL_WIDTH + 2 * sb_width:POOL_WIDTH + 3 * sb_width]
    for src, dst in ((q, q_ref), (k, k_ref), (v, v_ref)):
        for hh in range(n_heads):
            cols = src[:, hh * HEAD_DIM:(hh + 1) * HEAD_DIM].astype(BF16)
            for t in range(tm // ATT_BLK):
                dst[0, hh, t] = cols[t * ATT_BLK:(t + 1) * ATT_BLK]

    uext_ref[0:HALO, :] = jnp.where(s > 0, uext_ref[tm:tm + HALO, :], 0.0)

    u = proj[:, :POOL_WIDTH]
    uext_ref[HALO:HALO + tm, :] = u

    for src, dst in zip(cast_src, cast_dst):
        dst[...] = src[...].astype(BF16)

    pos = s * tm + lax.broadcasted_iota(jnp.int32, (tm, POOL_GROUP), 0)
    rows = HALO + tm
    mixed = []
    for g, w in enumerate(POOL_WINDOWS):
        c0, c1 = g * POOL_GROUP, (g + 1) * POOL_GROUP
        levels = w.bit_length() - 1
        start = HALO - SUBLANES * (levels - 1)
        win = uext_ref[start:rows, c0:c1] + uext_ref[start - 1:rows - 1, c0:c1]
        for lvl in range(2, levels + 1):
            back = 1 << (lvl - 1)
            lvl_ref[g, start:rows, :] = win
            start += SUBLANES
            win = win[SUBLANES:] + lvl_ref[g, start - back:rows - back, :]
        count = jnp.minimum(pos + 1, w).astype(F32)
        pooled = win / count - u[:, c0:c1]
        mixed.append(jnp.dot(pooled.astype(BF16), w_pm_ref[g], preferred_element_type=F32))
    y = jnp.concatenate(mixed, axis=-1) * pscale_ref[...]
    y_pool = jnp.dot(y.astype(BF16), w_brp_ref[...], preferred_element_type=F32)

    gy_ref[0] = (gates[:, :d_model] * y_pool).astype(BF16)


def _attn_kernel(q_ref, k_ref, v_ref, o_ref, carry_ref, acc_ref):
    n_heads, n_sub, blk, _ = q_ref.shape[1:]
    wide = 2 * blk

    def strict_lower(n):
        r = lax.broadcasted_iota(jnp.int32, (n, n), 0)
        c = lax.broadcasted_iota(jnp.int32, (n, n), 1)
        return jnp.where(r > c, 1.0, 0.0).astype(BF16)

    def scores(q, keys):
        return lax.dot_general(q, keys, (((1,), (1,)), ((), ())), preferred_element_type=F32)

    def log_terms(z):
        soft = jnp.log(1.0 + jnp.exp2(-jnp.abs(z))) * LOG2E
        lsz = jnp.minimum(z, 0.0) - soft
        return lsz, lsz - z

    half = blk // 2
    tri = strict_lower(blk)

    def causal_mask(rows_from, cols_from, shape):
        r = lax.broadcasted_iota(jnp.int32, shape, 0) + (rows_from + blk)
        c = lax.broadcasted_iota(jnp.int32, shape, 1) + cols_from
        return c < r

    causal_top = causal_mask(0, 0, (half, wide))
    causal_bot = causal_mask(half, half, (half, wide - half))

    def first_tile(sub):
        i = pl.program_id(1) * n_sub + sub
        prev = jnp.maximum(i - 1, 0)
        lsz_all, l1m_all = [], []
        for hh in range(n_heads):
            keys = jnp.concatenate([k_ref[0, hh, prev], k_ref[0, hh, i]], axis=0)
            lsz_t, l1m_t = log_terms(scores(q_ref[0, hh, sub, :half, :], keys))
            lsz_b, l1m_b = log_terms(scores(q_ref[0, hh, sub, half:, :], keys[half:]))
            lsz_all.append((lsz_t, lsz_b))
            l1m_all.append(jnp.concatenate(
                [jnp.where(causal_top, l1m_t, 0.0),
                 jnp.concatenate([jnp.zeros((half, half), F32),
                                  jnp.where(causal_bot, l1m_b, 0.0)], axis=1)], axis=0))
        stack = lambda lo: jnp.concatenate([l1m[:, lo:lo + blk] for l1m in l1m_all],
                                           axis=0).astype(BF16)
        after_prev_all = jnp.dot(stack(0), tri, preferred_element_type=F32)
        after_diag_all = jnp.dot(stack(blk), tri, preferred_element_type=F32)
        carries, accs = [], []
        for hh in range(n_heads):
            l1m = l1m_all[hh]
            lsz_t, lsz_b = lsz_all[hh]
            after_prev = after_prev_all[hh * blk:(hh + 1) * blk]
            after_diag = after_diag_all[hh * blk:(hh + 1) * blk]
            total_diag = after_diag[:, 0:1] + l1m[:, blk:blk + 1]
            after = jnp.concatenate([after_prev + total_diag, after_diag], axis=1)
            a_t = jnp.where(causal_top, jnp.exp2(lsz_t + after[:half]), 0.0)
            a_b = jnp.where(causal_bot, jnp.exp2(lsz_b + after[half:, half:]), 0.0)
            a = jnp.concatenate(
                [a_t, jnp.concatenate([jnp.zeros((half, half), F32), a_b], axis=1)], axis=0)
            v_prev = v_ref[0, hh, prev]
            if sub == 0:
                v_prev = jnp.where(i > 0, v_prev, jnp.zeros((), BF16))
            vals = jnp.concatenate([v_prev, v_ref[0, hh, i]], axis=0)
            accs.append(jnp.dot(a.astype(BF16), vals, preferred_element_type=F32))
            carries.append((after_prev[:half, 0:1] + total_diag[:half] + l1m[:half, 0:1],
                            after_prev[half:, half:half + 1] + total_diag[half:]
                            + l1m[half:, half:half + 1]))
        o_ref[0, sub * blk:(sub + 1) * blk, :] = jnp.concatenate(accs, axis=-1).astype(BF16)
        return i, carries, accs

    def any_alive(carries):
        return jnp.max(functools.reduce(jnp.maximum, carries)) > DEAD_LOG2

    def earlier_blocks(sub, i, carries, accs):
        late_alive = jnp.logical_and(i > 0, any_alive([late for _, late in carries]))
        early_alive = jnp.logical_and(i > 1, any_alive([early for early, _ in carries]))

        @pl.when(jnp.logical_or(late_alive, early_alive))
        def _():
            for hh in range(n_heads):
                carry_ref[hh, :half] = carries[hh][0]
                carry_ref[hh, half:] = carries[hh][1]
                acc_ref[hh] = accs[hh]

            tri_half = strict_lower(half)
            for hh in range(n_heads):
                lsz, l1m = log_terms(scores(q_ref[0, hh, sub, half:, :],
                                            k_ref[0, hh, i - 1, :half, :]))
                after = jnp.dot(l1m.astype(BF16), tri_half, preferred_element_type=F32)
                carry = carry_ref[hh, half:]
                a = jnp.exp2(lsz + after + carry)
                acc_ref[hh, half:] += jnp.dot(a.astype(BF16), v_ref[0, hh, i - 1, :half, :],
                                              preferred_element_type=F32)
                carry_ref[hh, half:] = carry + after[:, 0:1] + l1m[:, 0:1]

            def step(loop):
                it, _ = loop
                j = i - 2 - it
                new = []
                for hh in range(n_heads):
                    lsz, l1m = log_terms(scores(q_ref[0, hh, sub], k_ref[0, hh, j]))
                    after = jnp.dot(l1m.astype(BF16), tri, preferred_element_type=F32)
                    a = jnp.exp2(lsz + after + carry_ref[hh])
                    acc_ref[hh] += jnp.dot(a.astype(BF16), v_ref[0, hh, j],
                                           preferred_element_type=F32)
                    new.append(carry_ref[hh] + after[:, 0:1] + l1m[:, 0:1])
                    carry_ref[hh] = new[-1]
                return it + 1, any_alive(new)

            lax.while_loop(lambda loop: jnp.logical_and(loop[0] < i - 1, loop[1]),
                           step, (jnp.int32(0), True))
            o_ref[0, sub * blk:(sub + 1) * blk, :] = jnp.concatenate(
                [acc_ref[hh] for hh in range(n_heads)], axis=-1).astype(BF16)

    firsts = [first_tile(sub) for sub in range(n_sub)]
    for sub, (i, carries, accs) in enumerate(firsts):
        earlier_blocks(sub, i, carries, accs)


def _out_kernel(x_ref, osb_ref, gy_ref, gsb_ref, w_brs_ref, w_out_ref, g_post_mix_ref,
                g_pre_mlp_ref, w_up_ref, w_down_ref, g_post_mlp_ref, o_ref):
    def mixed(r):
        y_sb = jnp.dot(osb_ref[0, r], w_brs_ref[...], preferred_element_type=F32)
        merged = gy_ref[0, r].astype(F32) + gsb_ref[0, r].astype(F32) * y_sb
        return jnp.dot(merged.astype(BF16), w_out_ref[...], preferred_element_type=F32)

    def normed(r, mix):
        x1 = x_ref[0, r] + mix * _rms_scale(mix) * g_post_mix_ref[...]
        return x1, (x1 * _rms_scale(x1) * g_pre_mlp_ref[...]).astype(BF16)

    def mlp(h2):
        ff = None
        for c0 in range(0, w_up_ref.shape[1], FF_CHUNK):
            up = jnp.dot(h2, w_up_ref[:, c0:c0 + FF_CHUNK], preferred_element_type=F32)
            act = jnp.square(jnp.maximum(up, 0.0)).astype(BF16)
            part = jnp.dot(act, w_down_ref[c0:c0 + FF_CHUNK, :], preferred_element_type=F32)
            ff = part if ff is None else ff + part
        return ff

    half = x_ref.shape[1] // 2
    halves = (pl.ds(0, half), pl.ds(half, half))
    mixes = [mixed(r) for r in halves]
    normeds = [normed(r, mix) for r, mix in zip(halves, mixes)]
    for r, (x1, h2) in zip(halves, normeds):
        ff = mlp(h2)
        o_ref[0, r] = x1 + ff * _rms_scale(ff) * g_post_mlp_ref[...]


def _as_row(a):
    return a.reshape(1, -1).astype(F32)


def _tok_spec(width):
    return pl.BlockSpec((1, TOK_TILE, width), lambda b, s: (b, s, 0))


def _project(x, g_pre_mix, w_in, w_pool_mix, pool_scale, w_br_pool, w_gate, b_gate, n_heads,
             later_weights):
    B, S, D = x.shape
    sub = TOK_TILE // ATT_BLK
    n_tiles = S // TOK_TILE
    n_steps = B * n_tiles
    assert all(w.ndim == 2 and w.shape[0] % (BF16_ROWS * n_steps) == 0 for w in later_weights)
    slice_specs = [pl.BlockSpec((w.shape[0] // n_steps, w.shape[1]),
                                lambda b, s: (b * n_tiles + s, 0)) for w in later_weights]
    head_tile = pl.BlockSpec((1, n_heads, sub, ATT_BLK, HEAD_DIM), lambda b, s: (b, 0, s, 0, 0))
    head_shape = jax.ShapeDtypeStruct((B, n_heads, S // ATT_BLK, ATT_BLK, HEAD_DIM), BF16)
    return pl.pallas_call(
        functools.partial(_proj_kernel, n_heads=n_heads, d_model=D,
                          n_cast=len(later_weights)),
        grid=(B, n_tiles),
        in_specs=[
            _tok_spec(D),
            _const_spec((1, D)),
            _const_spec(w_in.shape),
            _const_spec(w_pool_mix.shape),
            _const_spec((1, POOL_WIDTH)),
            _const_spec(w_br_pool.shape),
            _const_spec(w_gate.shape),
            _const_spec((1, 2 * D)),
            *slice_specs,
        ],
        out_specs=[head_tile, head_tile, head_tile, _tok_spec(D), _tok_spec(D), *slice_specs],
        out_shape=[head_shape, head_shape, head_shape,
                   jax.ShapeDtypeStruct((B, S, D), BF16),
                   jax.ShapeDtypeStruct((B, S, D), BF16),
                   *[jax.ShapeDtypeStruct(w.shape, BF16) for w in later_weights]],
        scratch_shapes=[pltpu.VMEM((HALO + TOK_TILE, POOL_WIDTH), F32),
                        pltpu.VMEM((len(POOL_WINDOWS), HALO + TOK_TILE, POOL_GROUP), F32),
                        pltpu.VMEM(w_in.shape, BF16), pltpu.VMEM(w_pool_mix.shape, BF16),
                        pltpu.VMEM(w_br_pool.shape, BF16), pltpu.VMEM(w_gate.shape, BF16)],
        compiler_params=pltpu.CompilerParams(
            dimension_semantics=("arbitrary", "arbitrary"),
            vmem_limit_bytes=PROJ_VMEM_LIMIT_BYTES),
        name="proj_pool_gates",
    )(x, _as_row(g_pre_mix), w_in, w_pool_mix, _as_row(pool_scale), w_br_pool, w_gate,
      _as_row(b_gate), *later_weights)


def _attention(q, k, v):
    B, n_heads, n_blk = q.shape[:3]
    assert n_blk % ATT_QBLKS == 0
    q_spec = pl.BlockSpec((1, n_heads, ATT_QBLKS, ATT_BLK, HEAD_DIM),
                          lambda b, i: (b, 0, i, 0, 0))
    seq_spec = pl.BlockSpec((1, n_heads, n_blk, ATT_BLK, HEAD_DIM), lambda b, i: (b, 0, 0, 0, 0))
    sb_width = n_heads * HEAD_DIM
    return pl.pallas_call(
        _attn_kernel,
        grid=(B, n_blk // ATT_QBLKS),
        in_specs=[q_spec, seq_spec, seq_spec],
        out_specs=pl.BlockSpec((1, ATT_QBLKS * ATT_BLK, sb_width), lambda b, i: (b, i, 0)),
        out_shape=jax.ShapeDtypeStruct((B, n_blk * ATT_BLK, sb_width), BF16),
        scratch_shapes=[pltpu.VMEM((n_heads, ATT_BLK, 1), F32),
                        pltpu.VMEM((n_heads, ATT_BLK, HEAD_DIM), F32)],
        compiler_params=pltpu.CompilerParams(
            dimension_semantics=("parallel", "parallel"),
            vmem_limit_bytes=VMEM_LIMIT_BYTES),
        name="stickbreak_attn",
    )(q, k, v)


def _merge_mlp(x, o_sb, gy, gsb, w_br_sb, w_out, g_post_mix, g_pre_mlp, w_up, w_down,
               g_post_mlp):
    B, S, D = x.shape
    return pl.pallas_call(
        _out_kernel,
        grid=(B, S // TOK_TILE),
        in_specs=[
            _tok_spec(D),
            _tok_spec(o_sb.shape[-1]),
            _tok_spec(D),
            _tok_spec(D),
            _const_spec(w_br_sb.shape),
            _const_spec(w_out.shape),
            _const_spec((1, D)),
            _const_spec((1, D)),
            _const_spec(w_up.shape),
            _const_spec(w_down.shape),
            _const_spec((1, D)),
        ],
        out_specs=_tok_spec(D),
        out_shape=jax.ShapeDtypeStruct((B, S, D), x.dtype),
        compiler_params=pltpu.CompilerParams(
            dimension_semantics=("parallel", "parallel"),
            vmem_limit_bytes=VMEM_LIMIT_BYTES),
        name="merge_mlp",
    )(x, o_sb, gy, gsb, w_br_sb, w_out, _as_row(g_post_mix), _as_row(g_pre_mlp), w_up, w_down,
      _as_row(g_post_mlp))


def kernel(x, g_pre_mix, w_in, w_pool_mix, pool_scale, w_br_pool, w_br_sb, w_gate, b_gate,
           w_out, g_post_mix, g_pre_mlp, w_up, w_down, g_post_mlp):
    B, S, D = x.shape
    sb_width = w_br_sb.shape[0]
    assert w_in.shape[1] == POOL_WIDTH + 3 * sb_width and w_br_pool.shape[0] == POOL_WIDTH
    assert S % TOK_TILE == 0 and TOK_TILE % ATT_BLK == 0 and w_gate.shape[1] == 2 * D
    assert w_up.shape[1] % FF_CHUNK == 0

    q, k, v, gy, gsb, w_br_sb, w_out, w_up, w_down = _project(
        x, g_pre_mix, w_in, w_pool_mix, pool_scale, w_br_pool, w_gate, b_gate,
        sb_width // HEAD_DIM, later_weights=(w_br_sb, w_out, w_up, w_down))
    o_sb = _attention(q, k, v)
    return _merge_mlp(x, o_sb, gy, gsb, w_br_sb, w_out, g_post_mix, g_pre_mlp, w_up, w_down,
                      g_post_mlp)
```

```python
import functools

import jax
import jax.numpy as jnp
from jax import lax
from jax.experimental import pallas as pl
from jax.experimental.pallas import tpu as pltpu

RMS_EPS = 1e-6
POOL_WINDOWS = (2, 4, 8, 16)
POOL_GROUP = 128
POOL_WIDTH = POOL_GROUP * len(POOL_WINDOWS)
HEAD_DIM = 64
SUBLANES = 8
BF16_ROWS = 16
HALO = 4 * SUBLANES
TOK_TILE = 512
ATT_BLK = 256
ATT_QBLKS = 2
FF_CHUNK = 1024
DEAD_LOG2 = -151.0
LOG2E = 1.4426950408889634

VMEM_LIMIT_BYTES = 52 * 1024 * 1024
PROJ_VMEM_LIMIT_BYTES = 57 * 1024 * 1024

BF16 = jnp.bfloat16
F32 = jnp.float32


def _rms_scale(x):
    return lax.rsqrt(jnp.mean(x * x, axis=-1, keepdims=True) + RMS_EPS)


def _const_spec(shape):
    zeros = (0,) * len(shape)
    return pl.BlockSpec(shape, lambda *_: zeros, pipeline_mode=pl.Buffered(1))


def _proj_kernel(x_ref, g_pre_ref, w_in32_ref, w_pm32_ref, pscale_ref, w_brp32_ref,
                 w_gate32_ref, b_gate_ref, *rest, n_heads, d_model, n_cast):
    cast_src, rest = rest[:n_cast], rest[n_cast:]
    q_ref, k_ref, v_ref, gy_ref, gsb_ref = rest[:5]
    cast_dst, rest = rest[5:5 + n_cast], rest[5 + n_cast:]
    uext_ref, lvl_ref, w_in_ref, w_pm_ref, w_brp_ref, w_gate_ref = rest
    s = pl.program_id(1)
    tm = x_ref.shape[1]
    sb_width = n_heads * HEAD_DIM

    @pl.when(jnp.logical_and(pl.program_id(0) == 0, s == 0))
    def _():
        uext_ref[tm:tm + HALO, :] = jnp.zeros((HALO, POOL_WIDTH), F32)
        for src, dst in ((w_in32_ref, w_in_ref), (w_pm32_ref, w_pm_ref),
                         (w_brp32_ref, w_brp_ref), (w_gate32_ref, w_gate_ref)):
            dst[...] = src[...].astype(BF16)

    hs = []
    for r in (pl.ds(0, tm // 2), pl.ds(tm // 2, tm // 2)):
        x = x_ref[0, r]
        hs.append((x * _rms_scale(x) * g_pre_ref[...]).astype(BF16))
    both = lambda w_ref: jnp.concatenate(
        [jnp.dot(h, w_ref[...], preferred_element_type=F32) for h in hs], axis=0)
    proj = both(w_in_ref)
    gates = jax.nn.sigmoid(both(w_gate_ref) + b_gate_ref[...])
    gsb_ref[0] = gates[:, d_model:].astype(BF16)

    q = proj[:, POOL_WIDTH:POOL_WIDTH + sb_width] * (HEAD_DIM ** -0.5 * LOG2E)
    k = proj[:, POOL_WIDTH + sb_width:POOL_WIDTH + 2 * sb_width]
    v = proj[:, POOL_WIDTH + 2 * sb_width:POOL_WIDTH + 3 * sb_width]
    for src, dst in ((q, q_ref), (k, k_ref), (v, v_ref)):
        for hh in range(n_heads):
            cols = src[:, hh * HEAD_DIM:(hh + 1) * HEAD_DIM].astype(BF16)
            for t in range(tm // ATT_BLK):
                dst[0, hh, t] = cols[t * ATT_BLK:(t + 1) * ATT_BLK]

    uext_ref[0:HALO, :] = jnp.where(s > 0, uext_ref[tm:tm + HALO, :], 0.0)

    u = proj[:, :POOL_WIDTH]
    uext_ref[HALO:HALO + tm, :] = u

    for src, dst in zip(cast_src, cast_dst):
        dst[...] = src[...].astype(BF16)

    pos = s * tm + lax.broadcasted_iota(jnp.int32, (tm, POOL_GROUP), 0)
    rows = HALO + tm
    mixed = []
    for g, w in enumerate(POOL_WINDOWS):
        c0, c1 = g * POOL_GROUP, (g + 1) * POOL_GROUP
        levels = w.bit_length() - 1
        start = HALO - SUBLANES * (levels - 1)
        win = uext_ref[start:rows, c0:c1] + uext_ref[start - 1:rows - 1, c0:c1]
        for lvl in range(2, levels + 1):
            back = 1 << (lvl - 1)
            lvl_ref[g, start:rows, :] = win
            start += SUBLANES
            win = win[SUBLANES:] + lvl_ref[g, start - back:rows - back, :]
        count = jnp.minimum(pos + 1, w).astype(F32)
        pooled = win / count - u[:, c0:c1]
        mixed.append(jnp.dot(pooled.astype(BF16), w_pm_ref[g], preferred_element_type=F32))
    y = jnp.concatenate(mixed, axis=-1) * pscale_ref[...]
    y_pool = jnp.dot(y.astype(BF16), w_brp_ref[...], preferred_element_type=F32)

    gy_ref[0] = (gates[:, :d_model] * y_pool).astype(BF16)


def _attn_kernel(q_ref, k_ref, v_ref, o_ref, carry_ref, acc_ref):
    n_heads, n_sub, blk, _ = q_ref.shape[1:]
    wide = 2 * blk

    def strict_lower(n):
        r = lax.broadcasted_iota(jnp.int32, (n, n), 0)
        c = lax.broadcasted_iota(jnp.int32, (n, n), 1)
        return jnp.where(r > c, 1.0, 0.0).astype(BF16)

    def scores(q, keys):
        return lax.dot_general(q, keys, (((1,), (1,)), ((), ())), preferred_element_type=F32)

    def log_terms(z):
        soft = jnp.log(1.0 + jnp.exp2(-jnp.abs(z))) * LOG2E
        lsz = jnp.minimum(z, 0.0) - soft
        return lsz, lsz - z

    half = blk // 2
    quarter = blk // 4
    tri = strict_lower(blk)
    groups = ((0, 0), (quarter, half), (half, half), (half + quarter, blk))

    def causal_mask(row0, col0):
        shape = (quarter, wide - col0)
        r = lax.broadcasted_iota(jnp.int32, shape, 0) + (row0 + blk)
        c = lax.broadcasted_iota(jnp.int32, shape, 1) + col0
        return c < r

    masks = [causal_mask(row0, col0) for row0, col0 in groups]

    def pad_left(x, col0):
        return x if col0 == 0 else jnp.concatenate(
            [jnp.zeros((x.shape[0], col0), F32), x], axis=1)

    def first_tile(sub):
        i = pl.program_id(1) * n_sub + sub
        prev = jnp.maximum(i - 1, 0)
        lsz_all, l1m_all = [], []
        for hh in range(n_heads):
            keys = jnp.concatenate([k_ref[0, hh, prev], k_ref[0, hh, i]], axis=0)
            z_top = scores(q_ref[0, hh, sub, :half, :], keys)
            z_bot = scores(q_ref[0, hh, sub, half:, :], keys[half:])
            zs = (z_top[:quarter], z_top[quarter:, half:], z_bot[:quarter], z_bot[quarter:, half:])
            lszs, l1ms = [], []
            for z, mask, (_, col0) in zip(zs, masks, groups):
                lsz, l1m = log_terms(z)
                lszs.append(lsz)
                l1ms.append(pad_left(jnp.where(mask, l1m, 0.0), col0))
            lsz_all.append(lszs)
            l1m_all.append(jnp.concatenate(l1ms, axis=0))
        stack = lambda lo: jnp.concatenate([l1m[:, lo:lo + blk] for l1m in l1m_all],
                                           axis=0).astype(BF16)
        after_prev_all = jnp.dot(stack(0), tri, preferred_element_type=F32)
        after_diag_all = jnp.dot(stack(blk), tri, preferred_element_type=F32)
        carries, accs = [], []
        for hh in range(n_heads):
            l1m = l1m_all[hh]
            after_prev = after_prev_all[hh * blk:(hh + 1) * blk]
            after_diag = after_diag_all[hh * blk:(hh + 1) * blk]
            total_diag = after_diag[:, 0:1] + l1m[:, blk:blk + 1]
            after = jnp.concatenate([after_prev + total_diag, after_diag], axis=1)
            a_rows, carry_rows = [], []
            for lsz, mask, (row0, col0) in zip(lsz_all[hh], masks, groups):
                rows = slice(row0, row0 + quarter)
                a = jnp.where(mask, jnp.exp2(lsz + after[rows, col0:]), 0.0)
                a_rows.append(pad_left(a, col0))
                if col0 < blk:
                    carry_rows.append(after_prev[rows, col0:col0 + 1] + total_diag[rows]
                                      + l1m[rows, col0:col0 + 1])
                else:
                    carry_rows.append(total_diag[rows])
            v_prev = v_ref[0, hh, prev]
            if sub == 0:
                v_prev = jnp.where(i > 0, v_prev, jnp.zeros((), BF16))
            vals = jnp.concatenate([v_prev, v_ref[0, hh, i]], axis=0)
            accs.append(jnp.dot(jnp.concatenate(a_rows, axis=0).astype(BF16), vals,
                                preferred_element_type=F32))
            carries.append(carry_rows)
        o_ref[0, sub * blk:(sub + 1) * blk, :] = jnp.concatenate(accs, axis=-1).astype(BF16)
        return i, carries, accs

    def any_alive(carries):
        return jnp.max(functools.reduce(jnp.maximum, carries)) > DEAD_LOG2

    def earlier_blocks(sub, i, carries, accs):
        late_alive = jnp.logical_and(
            i > 0, any_alive([jnp.concatenate(c[1:], axis=1) for c in carries]))
        early_alive = jnp.logical_and(i > 1, any_alive([c[0] for c in carries]))

        @pl.when(jnp.logical_or(late_alive, early_alive))
        def _():
            for hh in range(n_heads):
                for (row0, _), carry in zip(groups, carries[hh]):
                    carry_ref[hh, row0:row0 + quarter] = carry
                acc_ref[hh] = accs[hh]

            tri_half = strict_lower(half)
            for rows, key_rows in ((slice(half + quarter, blk), slice(half, blk)),
                                   (slice(quarter, blk), slice(0, half))):
                for hh in range(n_heads):
                    lsz, l1m = log_terms(scores(q_ref[0, hh, sub, rows, :],
                                                k_ref[0, hh, i - 1, key_rows, :]))
                    after = jnp.dot(l1m.astype(BF16), tri_half, preferred_element_type=F32)
                    carry = carry_ref[hh, rows]
                    a = jnp.exp2(lsz + after + carry)
                    acc_ref[hh, rows] += jnp.dot(a.astype(BF16), v_ref[0, hh, i - 1, key_rows, :],
                                                 preferred_element_type=F32)
                    carry_ref[hh, rows] = carry + after[:, 0:1] + l1m[:, 0:1]

            def step(loop):
                it, _ = loop
                j = i - 2 - it
                new = []
                for hh in range(n_heads):
                    lsz, l1m = log_terms(scores(q_ref[0, hh, sub], k_ref[0, hh, j]))
                    after = jnp.dot(l1m.astype(BF16), tri, preferred_element_type=F32)
                    a = jnp.exp2(lsz + after + carry_ref[hh])
                    acc_ref[hh] += jnp.dot(a.astype(BF16), v_ref[0, hh, j],
                                           preferred_element_type=F32)
                    new.append(carry_ref[hh] + after[:, 0:1] + l1m[:, 0:1])
                    carry_ref[hh] = new[-1]
                return it + 1, any_alive(new)

            lax.while_loop(lambda loop: jnp.logical_and(loop[0] < i - 1, loop[1]),
                           step, (jnp.int32(0), True))
            o_ref[0, sub * blk:(sub + 1) * blk, :] = jnp.concatenate(
                [acc_ref[hh] for hh in range(n_heads)], axis=-1).astype(BF16)

    firsts = [first_tile(sub) for sub in range(n_sub)]
    for sub, (i, carries, accs) in enumerate(firsts):
        earlier_blocks(sub, i, carries, accs)


def _out_kernel(x_ref, osb_ref, gy_ref, gsb_ref, w_brs_ref, w_out_ref, g_post_mix_ref,
                g_pre_mlp_ref, w_up_ref, w_down_ref, g_post_mlp_ref, o_ref):
    def mixed(r):
        y_sb = jnp.dot(osb_ref[0, r], w_brs_ref[...], preferred_element_type=F32)
        merged = gy_ref[0, r].astype(F32) + gsb_ref[0, r].astype(F32) * y_sb
        return jnp.dot(merged.astype(BF16), w_out_ref[...], preferred_element_type=F32)

    def normed(r, mix):
        x1 = x_ref[0, r] + mix * _rms_scale(mix) * g_post_mix_ref[...]
        return x1, (x1 * _rms_scale(x1) * g_pre_mlp_ref[...]).astype(BF16)

    def mlp(h2):
        ff = None
        for c0 in range(0, w_up_ref.shape[1], FF_CHUNK):
            up = jnp.dot(h2, w_up_ref[:, c0:c0 + FF_CHUNK], preferred_element_type=F32)
            act = jnp.square(jnp.maximum(up, 0.0)).astype(BF16)
            part = jnp.dot(act, w_down_ref[c0:c0 + FF_CHUNK, :], preferred_element_type=F32)
            ff = part if ff is None else ff + part
        return ff

    half = x_ref.shape[1] // 2
    halves = (pl.ds(0, half), pl.ds(half, half))
    mixes = [mixed(r) for r in halves]
    normeds = [normed(r, mix) for r, mix in zip(halves, mixes)]
    for r, (x1, h2) in zip(halves, normeds):
        ff = mlp(h2)
        o_ref[0, r] = x1 + ff * _rms_scale(ff) * g_post_mlp_ref[...]


def _as_row(a):
    return a.reshape(1, -1).astype(F32)


def _tok_spec(width):
    return pl.BlockSpec((1, TOK_TILE, width), lambda b, s: (b, s, 0))


def _project(x, g_pre_mix, w_in, w_pool_mix, pool_scale, w_br_pool, w_gate, b_gate, n_heads,
             later_weights):
    B, S, D = x.shape
    sub = TOK_TILE // ATT_BLK
    n_tiles = S // TOK_TILE
    n_steps = B * n_tiles
    assert all(w.ndim == 2 and w.shape[0] % (BF16_ROWS * n_steps) == 0 for w in later_weights)
    slice_specs = [pl.BlockSpec((w.shape[0] // n_steps, w.shape[1]),
                                lambda b, s: (b * n_tiles + s, 0)) for w in later_weights]
    head_tile = pl.BlockSpec((1, n_heads, sub, ATT_BLK, HEAD_DIM), lambda b, s: (b, 0, s, 0, 0))
    head_shape = jax.ShapeDtypeStruct((B, n_heads, S // ATT_BLK, ATT_BLK, HEAD_DIM), BF16)
    return pl.pallas_call(
        functools.partial(_proj_kernel, n_heads=n_heads, d_model=D,
                          n_cast=len(later_weights)),
        grid=(B, n_tiles),
        in_specs=[
            _tok_spec(D),
            _const_spec((1, D)),
            _const_spec(w_in.shape),
            _const_spec(w_pool_mix.shape),
            _const_spec((1, POOL_WIDTH)),
            _const_spec(w_br_pool.shape),
            _const_spec(w_gate.shape),
            _const_spec((1, 2 * D)),
            *slice_specs,
        ],
        out_specs=[head_tile, head_tile, head_tile, _tok_spec(D), _tok_spec(D), *slice_specs],
        out_shape=[head_shape, head_shape, head_shape,
                   jax.ShapeDtypeStruct((B, S, D), BF16),
                   jax.ShapeDtypeStruct((B, S, D), BF16),
                   *[jax.ShapeDtypeStruct(w.shape, BF16) for w in later_weights]],
        scratch_shapes=[pltpu.VMEM((HALO + TOK_TILE, POOL_WIDTH), F32),
                        pltpu.VMEM((len(POOL_WINDOWS), HALO + TOK_TILE, POOL_GROUP), F32),
                        pltpu.VMEM(w_in.shape, BF16), pltpu.VMEM(w_pool_mix.shape, BF16),
                        pltpu.VMEM(w_br_pool.shape, BF16), pltpu.VMEM(w_gate.shape, BF16)],
        compiler_params=pltpu.CompilerParams(
            dimension_semantics=("arbitrary", "arbitrary"),
            vmem_limit_bytes=PROJ_VMEM_LIMIT_BYTES),
        name="proj_pool_gates",
    )(x, _as_row(g_pre_mix), w_in, w_pool_mix, _as_row(pool_scale), w_br_pool, w_gate,
      _as_row(b_gate), *later_weights)


def _attention(q, k, v):
    B, n_heads, n_blk = q.shape[:3]
    assert n_blk % ATT_QBLKS == 0
    q_spec = pl.BlockSpec((1, n_heads, ATT_QBLKS, ATT_BLK, HEAD_DIM),
                          lambda b, i: (b, 0, i, 0, 0))
    seq_spec = pl.BlockSpec((1, n_heads, n_blk, ATT_BLK, HEAD_DIM), lambda b, i: (b, 0, 0, 0, 0))
    sb_width = n_heads * HEAD_DIM
    return pl.pallas_call(
        _attn_kernel,
        grid=(B, n_blk // ATT_QBLKS),
        in_specs=[q_spec, seq_spec, seq_spec],
        out_specs=pl.BlockSpec((1, ATT_QBLKS * ATT_BLK, sb_width), lambda b, i: (b, i, 0)),
        out_shape=jax.ShapeDtypeStruct((B, n_blk * ATT_BLK, sb_width), BF16),
        scratch_shapes=[pltpu.VMEM((n_heads, ATT_BLK, 1), F32),
                        pltpu.VMEM((n_heads, ATT_BLK, HEAD_DIM), F32)],
        compiler_params=pltpu.CompilerParams(
            dimension_semantics=("parallel", "parallel"),
            vmem_limit_bytes=VMEM_LIMIT_BYTES),
        name="stickbreak_attn",
    )(q, k, v)


def _merge_mlp(x, o_sb, gy, gsb, w_br_sb, w_out, g_post_mix, g_pre_mlp, w_up, w_down,
               g_post_mlp):
    B, S, D = x.shape
    return pl.pallas_call(
        _out_kernel,
        grid=(B, S // TOK_TILE),
        in_specs=[
            _tok_spec(D),
            _tok_spec(o_sb.shape[-1]),
            _tok_spec(D),
            _tok_spec(D),
            _const_spec(w_br_sb.shape),
            _const_spec(w_out.shape),
            _const_spec((1, D)),
            _const_spec((1, D)),
            _const_spec(w_up.shape),
            _const_spec(w_down.shape),
            _const_spec((1, D)),
        ],
        out_specs=_tok_spec(D),
        out_shape=jax.ShapeDtypeStruct((B, S, D), x.dtype),
        compiler_params=pltpu.CompilerParams(
            dimension_semantics=("parallel", "parallel"),
            vmem_limit_bytes=VMEM_LIMIT_BYTES),
        name="merge_mlp",
    )(x, o_sb, gy, gsb, w_br_sb, w_out, _as_row(g_post_mix), _as_row(g_pre_mlp), w_up, w_down,
      _as_row(g_post_mlp))


def kernel(x, g_pre_mix, w_in, w_pool_mix, pool_scale, w_br_pool, w_br_sb, w_gate, b_gate,
           w_out, g_post_mix, g_pre_mlp, w_up, w_down, g_post_mlp):
    B, S, D = x.shape
    sb_width = w_br_sb.shape[0]
    assert w_in.shape[1] == POOL_WIDTH + 3 * sb_width and w_br_pool.shape[0] == POOL_WIDTH
    assert S % TOK_TILE == 0 and TOK_TILE % ATT_BLK == 0 and w_gate.shape[1] == 2 * D
    assert w_up.shape[1] % FF_CHUNK == 0

    q, k, v, gy, gsb, w_br_sb, w_out, w_up, w_down = _project(
        x, g_pre_mix, w_in, w_pool_mix, pool_scale, w_br_pool, w_gate, b_gate,
        sb_width // HEAD_DIM, later_weights=(w_br_sb, w_out, w_up, w_down))
    o_sb = _attention(q, k, v)
    return _merge_mlp(x, o_sb, gy, gsb, w_br_sb, w_out, g_post_mix, g_pre_mlp, w_up, w_down,
                      g_post_mlp)
```

```python
import functools

import jax
import jax.numpy as jnp
from jax import lax
from jax.experimental import pallas as pl
from jax.experimental.pallas import tpu as pltpu

RMS_EPS = 1e-6
POOL_WINDOWS = (2, 4, 8, 16)
POOL_GROUP = 128
POOL_WIDTH = POOL_GROUP * len(POOL_WINDOWS)
HEAD_DIM = 64
SUBLANES = 8
BF16_ROWS = 16
HALO = 4 * SUBLANES
TOK_TILE = 512
ATT_BLK = 256
ATT_QBLKS = 2
FF_CHUNK = 1024
DEAD_LOG2 = -151.0
LOG2E = 1.4426950408889634

VMEM_LIMIT_BYTES = 52 * 1024 * 1024
PROJ_VMEM_LIMIT_BYTES = 57 * 1024 * 1024

BF16 = jnp.bfloat16
F32 = jnp.float32


def _rms_scale(x):
    return lax.rsqrt(jnp.mean(x * x, axis=-1, keepdims=True) + RMS_EPS)


def _const_spec(shape):
    zeros = (0,) * len(shape)
    return pl.BlockSpec(shape, lambda *_: zeros, pipeline_mode=pl.Buffered(1))


def _proj_kernel(x_ref, g_pre_ref, w_in32_ref, w_pm32_ref, pscale_ref, w_brp32_ref,
                 w_gate32_ref, b_gate_ref, *rest, n_heads, d_model, n_cast):
    cast_src, rest = rest[:n_cast], rest[n_cast:]
    q_ref, k_ref, v_ref, gy_ref, gsb_ref = rest[:5]
    cast_dst, rest = rest[5:5 + n_cast], rest[5 + n_cast:]
    uext_ref, lvl_ref, w_in_ref, w_pm_ref, w_brp_ref, w_gate_ref = rest
    s = pl.program_id(1)
    tm = x_ref.shape[1]
    sb_width = n_heads * HEAD_DIM

    @pl.when(jnp.logical_and(pl.program_id(0) == 0, s == 0))
    def _():
        uext_ref[tm:tm + HALO, :] = jnp.zeros((HALO, POOL_WIDTH), F32)
        for src, dst in ((w_in32_ref, w_in_ref), (w_pm32_ref, w_pm_ref),
                         (w_brp32_ref, w_brp_ref), (w_gate32_ref, w_gate_ref)):
            dst[...] = src[...].astype(BF16)

    hs = []
    for r in (pl.ds(0, tm // 2), pl.ds(tm // 2, tm // 2)):
        x = x_ref[0, r]
        hs.append((x * _rms_scale(x) * g_pre_ref[...]).astype(BF16))
    both = lambda w_ref: jnp.concatenate(
        [jnp.dot(h, w_ref[...], preferred_element_type=F32) for h in hs], axis=0)
    proj = both(w_in_ref)
    gates = jax.nn.sigmoid(both(w_gate_ref) + b_gate_ref[...])
    gsb_ref[0] = gates[:, d_model:].astype(BF16)

    q = proj[:, POOL_WIDTH:POOL_WIDTH + sb_width] * (HEAD_DIM ** -0.5 * LOG2E)
    k = proj[:, POOL_WIDTH + sb_width:POOL_WIDTH + 2 * sb_width]
    v = proj[:, POOL_WIDTH + 2 * sb_width:POOL_WIDTH + 3 * sb_width]
    for src, dst in ((q, q_ref), (k, k_ref), (v, v_ref)):
        for hh in range(n_heads):
            cols = src[:, hh * HEAD_DIM:(hh + 1) * HEAD_DIM].astype(BF16)
            for t in range(tm // ATT_BLK):
                dst[0, hh, t] = cols[t * ATT_BLK:(t + 1) * ATT_BLK]

    uext_ref[0:HALO, :] = jnp.where(s > 0, uext_ref[tm:tm + HALO, :], 0.0)

    u = proj[:, :POOL_WIDTH]
    uext_ref[HALO:HALO + tm, :] = u

    for src, dst in zip(cast_src, cast_dst):
        dst[...] = src[...].astype(BF16)

    pos = s * tm + lax.broadcasted_iota(jnp.int32, (tm, POOL_GROUP), 0)
    rows = HALO + tm
    mixed = []
    for g, w in enumerate(POOL_WINDOWS):
        c0, c1 = g * POOL_GROUP, (g + 1) * POOL_GROUP
        levels = w.bit_length() - 1
        start = HALO - SUBLANES * (levels - 1)
        win = uext_ref[start:rows, c0:c1] + uext_ref[start - 1:rows - 1, c0:c1]
        for lvl in range(2, levels + 1):
            back = 1 << (lvl - 1)
            lvl_ref[g, start:rows, :] = win
            start += SUBLANES
            win = win[SUBLANES:] + lvl_ref[g, start - back:rows - back, :]
        count = jnp.minimum(pos + 1, w).astype(F32)
        pooled = win / count - u[:, c0:c1]
        mixed.append(jnp.dot(pooled.astype(BF16), w_pm_ref[g], preferred_element_type=F32))
    y = jnp.concatenate(mixed, axis=-1) * pscale_ref[...]
    y_pool = jnp.dot(y.astype(BF16), w_brp_ref[...], preferred_element_type=F32)

    gy_ref[0] = (gates[:, :d_model] * y_pool).astype(BF16)


def _attn_kernel(q_ref, k_ref, v_ref, o_ref, carry_ref, acc_ref):
    n_heads, n_sub, blk, _ = q_ref.shape[1:]
    wide = 2 * blk

    def strict_lower(n):
        r = lax.broadcasted_iota(jnp.int32, (n, n), 0)
        c = lax.broadcasted_iota(jnp.int32, (n, n), 1)
        return jnp.where(r > c, 1.0, 0.0).astype(BF16)

    def scores(q, keys):
        return lax.dot_general(q, keys, (((1,), (1,)), ((), ())), preferred_element_type=F32)

    def log_terms(z):
        soft = jnp.log(1.0 + jnp.exp2(-jnp.abs(z))) * LOG2E
        lsz = jnp.minimum(z, 0.0) - soft
        return lsz, lsz - z

    half = blk // 2
    quarter = blk // 4
    tri = strict_lower(blk)
    groups = ((0, 0), (quarter, half), (half, half), (half + quarter, blk))

    def causal_mask(row0, col0):
        shape = (quarter, wide - col0)
        r = lax.broadcasted_iota(jnp.int32, shape, 0) + (row0 + blk)
        c = lax.broadcasted_iota(jnp.int32, shape, 1) + col0
        return c < r

    masks = [causal_mask(row0, col0) for row0, col0 in groups]

    def pad_left(x, col0):
        return x if col0 == 0 else jnp.concatenate(
            [jnp.zeros((x.shape[0], col0), F32), x], axis=1)

    def first_tile(sub):
        i = pl.program_id(1) * n_sub + sub
        prev = jnp.maximum(i - 1, 0)
        lsz_all, l1m_all = [], []
        for hh in range(n_heads):
            keys = jnp.concatenate([k_ref[0, hh, prev], k_ref[0, hh, i]], axis=0)
            z_top = scores(q_ref[0, hh, sub, :half, :], keys)
            z_bot = scores(q_ref[0, hh, sub, half:, :], keys[half:])
            zs = (z_top[:quarter], z_top[quarter:, half:], z_bot[:quarter], z_bot[quarter:, half:])
            lszs, l1ms = [], []
            for z, mask, (_, col0) in zip(zs, masks, groups):
                lsz, l1m = log_terms(z)
                lszs.append(lsz)
                l1ms.append(pad_left(jnp.where(mask, l1m, 0.0), col0))
            lsz_all.append(lszs)
            l1m_all.append(jnp.concatenate(l1ms, axis=0))
        stack = lambda lo: jnp.concatenate([l1m[:, lo:lo + blk] for l1m in l1m_all],
                                           axis=0).astype(BF16)
        after_prev_all = jnp.dot(stack(0), tri, preferred_element_type=F32)
        after_diag_all = jnp.dot(stack(blk), tri, preferred_element_type=F32)
        carries, accs = [], []
        for hh in range(n_heads):
            l1m = l1m_all[hh]
            after_prev = after_prev_all[hh * blk:(hh + 1) * blk]
            after_diag = after_diag_all[hh * blk:(hh + 1) * blk]
            total_diag = after_diag[:, 0:1] + l1m[:, blk:blk + 1]
            after = jnp.concatenate([after_prev + total_diag, after_diag], axis=1)
            a_rows, carry_rows = [], []
            for lsz, mask, (row0, col0) in zip(lsz_all[hh], masks, groups):
                rows = slice(row0, row0 + quarter)
                a = jnp.where(mask, jnp.exp2(lsz + after[rows, col0:]), 0.0)
                a_rows.append(pad_left(a, col0))
                if col0 < blk:
                    carry_rows.append(after_prev[rows, col0:col0 + 1] + total_diag[rows]
                                      + l1m[rows, col0:col0 + 1])
                else:
                    carry_rows.append(total_diag[rows])
            v_prev = v_ref[0, hh, prev]
            if sub == 0:
                v_prev = jnp.where(i > 0, v_prev, jnp.zeros((), BF16))
            vals = jnp.concatenate([v_prev, v_ref[0, hh, i]], axis=0)
            accs.append(jnp.dot(jnp.concatenate(a_rows, axis=0).astype(BF16), vals,
                                preferred_element_type=F32))
            carries.append(carry_rows)
        o_ref[0, sub * blk:(sub + 1) * blk, :] = jnp.concatenate(accs, axis=-1).astype(BF16)
        return i, carries, accs

    def any_alive(carries):
        return jnp.max(functools.reduce(jnp.maximum, carries)) > DEAD_LOG2

    def earlier_blocks(sub, i, carries, accs):
        late_alive = jnp.logical_and(
            i > 0, any_alive([jnp.concatenate(c[1:], axis=1) for c in carries]))
        early_alive = jnp.logical_and(i > 1, any_alive([c[0] for c in carries]))

        @pl.when(jnp.logical_or(late_alive, early_alive))
        def _():
            for hh in range(n_heads):
                for (row0, _), carry in zip(groups, carries[hh]):
                    carry_ref[hh, row0:row0 + quarter] = carry
                acc_ref[hh] = accs[hh]

            tri_half = strict_lower(half)
            for rows, key_rows in ((slice(half + quarter, blk), slice(half, blk)),
                                   (slice(quarter, blk), slice(0, half))):
                def skipped_part(hh, _, rows=rows, key_rows=key_rows):
                    lsz, l1m = log_terms(scores(q_ref[0, hh, sub, rows, :],
                                                k_ref[0, hh, i - 1, key_rows, :]))
                    after = jnp.dot(l1m.astype(BF16), tri_half, preferred_element_type=F32)
                    carry = carry_ref[hh, rows]
                    a = jnp.exp2(lsz + after + carry)
                    acc_ref[hh, rows] += jnp.dot(a.astype(BF16), v_ref[0, hh, i - 1, key_rows, :],
                                                 preferred_element_type=F32)
                    carry_ref[hh, rows] = carry + after[:, 0:1] + l1m[:, 0:1]
                    return 0

                lax.fori_loop(0, n_heads, skipped_part, 0)

            def step(loop):
                it, _ = loop
                j = i - 2 - it

                def one_head(hh, worst):
                    lsz, l1m = log_terms(scores(q_ref[0, hh, sub], k_ref[0, hh, j]))
                    after = jnp.dot(l1m.astype(BF16), tri, preferred_element_type=F32)
                    carry = carry_ref[hh]
                    a = jnp.exp2(lsz + after + carry)
                    acc_ref[hh] += jnp.dot(a.astype(BF16), v_ref[0, hh, j],
                                           preferred_element_type=F32)
                    carry = carry + after[:, 0:1] + l1m[:, 0:1]
                    carry_ref[hh] = carry
                    return jnp.maximum(worst, carry)

                worst = lax.fori_loop(0, n_heads, one_head, jnp.full((blk, 1), 2 * DEAD_LOG2, F32))
                return it + 1, any_alive([worst])

            lax.while_loop(lambda loop: jnp.logical_and(loop[0] < i - 1, loop[1]),
                           step, (jnp.int32(0), True))
            o_ref[0, sub * blk:(sub + 1) * blk, :] = jnp.concatenate(
                [acc_ref[hh] for hh in range(n_heads)], axis=-1).astype(BF16)

    firsts = [first_tile(sub) for sub in range(n_sub)]
    for sub, (i, carries, accs) in enumerate(firsts):
        earlier_blocks(sub, i, carries, accs)


def _out_kernel(x_ref, osb_ref, gy_ref, gsb_ref, w_brs_ref, w_out_ref, g_post_mix_ref,
                g_pre_mlp_ref, w_up_ref, w_down_ref, g_post_mlp_ref, o_ref):
    def mixed(r):
        y_sb = jnp.dot(osb_ref[0, r], w_brs_ref[...], preferred_element_type=F32)
        merged = gy_ref[0, r].astype(F32) + gsb_ref[0, r].astype(F32) * y_sb
        return jnp.dot(merged.astype(BF16), w_out_ref[...], preferred_element_type=F32)

    def normed(r, mix):
        x1 = x_ref[0, r] + mix * _rms_scale(mix) * g_post_mix_ref[...]
        return x1, (x1 * _rms_scale(x1) * g_pre_mlp_ref[...]).astype(BF16)

    def mlp(h2):
        ff = None
        for c0 in range(0, w_up_ref.shape[1], FF_CHUNK):
            up = jnp.dot(h2, w_up_ref[:, c0:c0 + FF_CHUNK], preferred_element_type=F32)
            act = jnp.square(jnp.maximum(up, 0.0)).astype(BF16)
            part = jnp.dot(act, w_down_ref[c0:c0 + FF_CHUNK, :], preferred_element_type=F32)
            ff = part if ff is None else ff + part
        return ff

    half = x_ref.shape[1] // 2
    halves = (pl.ds(0, half), pl.ds(half, half))
    mixes = [mixed(r) for r in halves]
    normeds = [normed(r, mix) for r, mix in zip(halves, mixes)]
    for r, (x1, h2) in zip(halves, normeds):
        ff = mlp(h2)
        o_ref[0, r] = x1 + ff * _rms_scale(ff) * g_post_mlp_ref[...]


def _as_row(a):
    return a.reshape(1, -1).astype(F32)


def _tok_spec(width):
    return pl.BlockSpec((1, TOK_TILE, width), lambda b, s: (b, s, 0))


def _project(x, g_pre_mix, w_in, w_pool_mix, pool_scale, w_br_pool, w_gate, b_gate, n_heads,
             later_weights):
    B, S, D = x.shape
    sub = TOK_TILE // ATT_BLK
    n_tiles = S // TOK_TILE
    n_steps = B * n_tiles
    assert all(w.ndim == 2 and w.shape[0] % (BF16_ROWS * n_steps) == 0 for w in later_weights)
    slice_specs = [pl.BlockSpec((w.shape[0] // n_steps, w.shape[1]),
                                lambda b, s: (b * n_tiles + s, 0)) for w in later_weights]
    head_tile = pl.BlockSpec((1, n_heads, sub, ATT_BLK, HEAD_DIM), lambda b, s: (b, 0, s, 0, 0))
    head_shape = jax.ShapeDtypeStruct((B, n_heads, S // ATT_BLK, ATT_BLK, HEAD_DIM), BF16)
    return pl.pallas_call(
        functools.partial(_proj_kernel, n_heads=n_heads, d_model=D,
                          n_cast=len(later_weights)),
        grid=(B, n_tiles),
        in_specs=[
            _tok_spec(D),
            _const_spec((1, D)),
            _const_spec(w_in.shape),
            _const_spec(w_pool_mix.shape),
            _const_spec((1, POOL_WIDTH)),
            _const_spec(w_br_pool.shape),
            _const_spec(w_gate.shape),
            _const_spec((1, 2 * D)),
            *slice_specs,
        ],
        out_specs=[head_tile, head_tile, head_tile, _tok_spec(D), _tok_spec(D), *slice_specs],
        out_shape=[head_shape, head_shape, head_shape,
                   jax.ShapeDtypeStruct((B, S, D), BF16),
                   jax.ShapeDtypeStruct((B, S, D), BF16),
                   *[jax.ShapeDtypeStruct(w.shape, BF16) for w in later_weights]],
        scratch_shapes=[pltpu.VMEM((HALO + TOK_TILE, POOL_WIDTH), F32),
                        pltpu.VMEM((len(POOL_WINDOWS), HALO + TOK_TILE, POOL_GROUP), F32),
                        pltpu.VMEM(w_in.shape, BF16), pltpu.VMEM(w_pool_mix.shape, BF16),
                        pltpu.VMEM(w_br_pool.shape, BF16), pltpu.VMEM(w_gate.shape, BF16)],
        compiler_params=pltpu.CompilerParams(
            dimension_semantics=("arbitrary", "arbitrary"),
            vmem_limit_bytes=PROJ_VMEM_LIMIT_BYTES),
        name="proj_pool_gates",
    )(x, _as_row(g_pre_mix), w_in, w_pool_mix, _as_row(pool_scale), w_br_pool, w_gate,
      _as_row(b_gate), *later_weights)


def _attention(q, k, v):
    B, n_heads, n_blk = q.shape[:3]
    assert n_blk % ATT_QBLKS == 0
    q_spec = pl.BlockSpec((1, n_heads, ATT_QBLKS, ATT_BLK, HEAD_DIM),
                          lambda b, i: (b, 0, i, 0, 0))
    seq_spec = pl.BlockSpec((1, n_heads, n_blk, ATT_BLK, HEAD_DIM), lambda b, i: (b, 0, 0, 0, 0))
    sb_width = n_heads * HEAD_DIM
    return pl.pallas_call(
        _attn_kernel,
        grid=(B, n_blk // ATT_QBLKS),
        in_specs=[q_spec, seq_spec, seq_spec],
        out_specs=pl.BlockSpec((1, ATT_QBLKS * ATT_BLK, sb_width), lambda b, i: (b, i, 0)),
        out_shape=jax.ShapeDtypeStruct((B, n_blk * ATT_BLK, sb_width), BF16),
        scratch_shapes=[pltpu.VMEM((n_heads, ATT_BLK, 1), F32),
                        pltpu.VMEM((n_heads, ATT_BLK, HEAD_DIM), F32)],
        compiler_params=pltpu.CompilerParams(
            dimension_semantics=("parallel", "parallel"),
            vmem_limit_bytes=VMEM_LIMIT_BYTES),
        name="stickbreak_attn",
    )(q, k, v)


def _merge_mlp(x, o_sb, gy, gsb, w_br_sb, w_out, g_post_mix, g_pre_mlp, w_up, w_down,
               g_post_mlp):
    B, S, D = x.shape
    return pl.pallas_call(
        _out_kernel,
        grid=(B, S // TOK_TILE),
        in_specs=[
            _tok_spec(D),
            _tok_spec(o_sb.shape[-1]),
            _tok_spec(D),
            _tok_spec(D),
            _const_spec(w_br_sb.shape),
            _const_spec(w_out.shape),
            _const_spec((1, D)),
            _const_spec((1, D)),
            _const_spec(w_up.shape),
            _const_spec(w_down.shape),
            _const_spec((1, D)),
        ],
        out_specs=_tok_spec(D),
        out_shape=jax.ShapeDtypeStruct((B, S, D), x.dtype),
        compiler_params=pltpu.CompilerParams(
            dimension_semantics=("parallel", "parallel"),
            vmem_limit_bytes=VMEM_LIMIT_BYTES),
        name="merge_mlp",
    )(x, o_sb, gy, gsb, w_br_sb, w_out, _as_row(g_post_mix), _as_row(g_pre_mlp), w_up, w_down,
      _as_row(g_post_mlp))


def kernel(x, g_pre_mix, w_in, w_pool_mix, pool_scale, w_br_pool, w_br_sb, w_gate, b_gate,
           w_out, g_post_mix, g_pre_mlp, w_up, w_down, g_post_mlp):
    B, S, D = x.shape
    sb_width = w_br_sb.shape[0]
    assert w_in.shape[1] == POOL_WIDTH + 3 * sb_width and w_br_pool.shape[0] == POOL_WIDTH
    assert S % TOK_TILE == 0 and TOK_TILE % ATT_BLK == 0 and w_gate.shape[1] == 2 * D
    assert w_up.shape[1] % FF_CHUNK == 0

    q, k, v, gy, gsb, w_br_sb, w_out, w_up, w_down = _project(
        x, g_pre_mix, w_in, w_pool_mix, pool_scale, w_br_pool, w_gate, b_gate,
        sb_width // HEAD_DIM, later_weights=(w_br_sb, w_out, w_up, w_down))
    o_sb = _attention(q, k, v)
    return _merge_mlp(x, o_sb, gy, gsb, w_br_sb, w_out, g_post_mix, g_pre_mlp, w_up, w_down,
                      g_post_mlp)
```

```python
import functools

import jax
import jax.numpy as jnp
from jax import lax
from jax.experimental import pallas as pl
from jax.experimental.pallas import tpu as pltpu

RMS_EPS = 1e-6
POOL_WINDOWS = (2, 4, 8, 16)
POOL_GROUP = 128
POOL_WIDTH = POOL_GROUP * len(POOL_WINDOWS)
HEAD_DIM = 64
SUBLANES = 8
BF16_ROWS = 16
HALO = 4 * SUBLANES
TOK_TILE = 512
ATT_BLK = 256
ATT_QBLKS = 2
FF_CHUNK = 1024
DEAD_LOG2 = -151.0
LOG2E = 1.4426950408889634

VMEM_LIMIT_BYTES = 52 * 1024 * 1024
PROJ_VMEM_LIMIT_BYTES = 57 * 1024 * 1024

BF16 = jnp.bfloat16
F32 = jnp.float32


def _rms_scale(x):
    return lax.rsqrt(jnp.mean(x * x, axis=-1, keepdims=True) + RMS_EPS)


def _const_spec(shape):
    zeros = (0,) * len(shape)
    return pl.BlockSpec(shape, lambda *_: zeros, pipeline_mode=pl.Buffered(1))


def _proj_kernel(x_ref, g_pre_ref, w_in32_ref, w_pm32_ref, pscale_ref, w_brp32_ref,
                 w_gate32_ref, b_gate_ref, *rest, n_heads, d_model, n_cast):
    cast_src, rest = rest[:n_cast], rest[n_cast:]
    q_ref, k_ref, v_ref, gy_ref, gsb_ref = rest[:5]
    cast_dst, rest = rest[5:5 + n_cast], rest[5 + n_cast:]
    uext_ref, lvl_ref, w_in_ref, w_pm_ref, w_brp_ref, w_gate_ref = rest
    s = pl.program_id(1)
    tm = x_ref.shape[1]
    sb_width = n_heads * HEAD_DIM

    @pl.when(jnp.logical_and(pl.program_id(0) == 0, s == 0))
    def _():
        uext_ref[tm:tm + HALO, :] = jnp.zeros((HALO, POOL_WIDTH), F32)
        for src, dst in ((w_in32_ref, w_in_ref), (w_pm32_ref, w_pm_ref),
                         (w_brp32_ref, w_brp_ref), (w_gate32_ref, w_gate_ref)):
            dst[...] = src[...].astype(BF16)

    hs = []
    for r in (pl.ds(0, tm // 2), pl.ds(tm // 2, tm // 2)):
        x = x_ref[0, r]
        hs.append((x * _rms_scale(x) * g_pre_ref[...]).astype(BF16))
    both = lambda w_ref: jnp.concatenate(
        [jnp.dot(h, w_ref[...], preferred_element_type=F32) for h in hs], axis=0)
    proj = both(w_in_ref)
    gates = jax.nn.sigmoid(both(w_gate_ref) + b_gate_ref[...])
    gsb_ref[0] = gates[:, d_model:].astype(BF16)

    q = proj[:, POOL_WIDTH:POOL_WIDTH + sb_width] * (HEAD_DIM ** -0.5 * LOG2E)
    k = proj[:, POOL_WIDTH + sb_width:POOL_WIDTH + 2 * sb_width]
    v = proj[:, POOL_WIDTH + 2 * sb_width:POOL_WIDTH + 3 * sb_width]
    for src, dst in ((q, q_ref), (k, k_ref), (v, v_ref)):
        for hh in range(n_heads):
            cols = src[:, hh * HEAD_DIM:(hh + 1) * HEAD_DIM].astype(BF16)
            for t in range(tm // ATT_BLK):
                dst[0, hh, t] = cols[t * ATT_BLK:(t + 1) * ATT_BLK]

    uext_ref[0:HALO, :] = jnp.where(s > 0, uext_ref[tm:tm + HALO, :], 0.0)

    u = proj[:, :POOL_WIDTH]
    uext_ref[HALO:HALO + tm, :] = u

    for src, dst in zip(cast_src, cast_dst):
        dst[...] = src[...].astype(BF16)

    pos = s * tm + lax.broadcasted_iota(jnp.int32, (tm, POOL_GROUP), 0)
    rows = HALO + tm
    mixed = []
    for g, w in enumerate(POOL_WINDOWS):
        c0, c1 = g * POOL_GROUP, (g + 1) * POOL_GROUP
        levels = w.bit_length() - 1
        start = HALO - SUBLANES * (levels - 1)
        win = uext_ref[start:rows, c0:c1] + uext_ref[start - 1:rows - 1, c0:c1]
        for lvl in range(2, levels + 1):
            back = 1 << (lvl - 1)
            lvl_ref[g, start:rows, :] = win
            start += SUBLANES
            win = win[SUBLANES:] + lvl_ref[g, start - back:rows - back, :]
        count = jnp.minimum(pos + 1, w).astype(F32)
        pooled = win / count - u[:, c0:c1]
        mixed.append(jnp.dot(pooled.astype(BF16), w_pm_ref[g], preferred_element_type=F32))
    y = jnp.concatenate(mixed, axis=-1) * pscale_ref[...]
    y_pool = jnp.dot(y.astype(BF16), w_brp_ref[...], preferred_element_type=F32)

    gy_ref[0] = (gates[:, :d_model] * y_pool).astype(BF16)


def _attn_kernel(q_ref, k_ref, v_ref, o_ref, carry_ref, acc_ref):
    n_heads, n_sub, blk, _ = q_ref.shape[1:]
    wide = 2 * blk

    def strict_lower(n):
        r = lax.broadcasted_iota(jnp.int32, (n, n), 0)
        c = lax.broadcasted_iota(jnp.int32, (n, n), 1)
        return jnp.where(r > c, 1.0, 0.0).astype(BF16)

    def scores(q, keys):
        return lax.dot_general(q, keys, (((1,), (1,)), ((), ())), preferred_element_type=F32)

    def log_terms(z):
        soft = jnp.log(1.0 + jnp.exp2(-jnp.abs(z))) * LOG2E
        lsz = jnp.minimum(z, 0.0) - soft
        return lsz, lsz - z

    half = blk // 2
    quarter = blk // 4
    tri = strict_lower(blk)
    groups = ((0, 0), (quarter, half), (half, half), (half + quarter, blk))

    def causal_mask(row0, col0):
        shape = (quarter, wide - col0)
        r = lax.broadcasted_iota(jnp.int32, shape, 0) + (row0 + blk)
        c = lax.broadcasted_iota(jnp.int32, shape, 1) + col0
        return c < r

    masks = [causal_mask(row0, col0) for row0, col0 in groups]

    def pad_left(x, col0):
        return x if col0 == 0 else jnp.concatenate(
            [jnp.zeros((x.shape[0], col0), F32), x], axis=1)

    def first_tile(sub):
        i = pl.program_id(1) * n_sub + sub
        prev = jnp.maximum(i - 1, 0)
        lsz_all, l1m_all = [], []
        for hh in range(n_heads):
            keys = jnp.concatenate([k_ref[0, hh, prev], k_ref[0, hh, i]], axis=0)
            z_top = scores(q_ref[0, hh, sub, :half, :], keys)
            z_bot = scores(q_ref[0, hh, sub, half:, :], keys[half:])
            zs = (z_top[:quarter], z_top[quarter:, half:], z_bot[:quarter], z_bot[quarter:, half:])
            lszs, l1ms = [], []
            for z, mask, (_, col0) in zip(zs, masks, groups):
                lsz, l1m = log_terms(z)
                lszs.append(lsz)
                l1ms.append(pad_left(jnp.where(mask, l1m, 0.0), col0))
            lsz_all.append(lszs)
            l1m_all.append(jnp.concatenate(l1ms, axis=0))
        n_prev = half + quarter
        stack = lambda lo, n: jnp.concatenate([l1m[:n, lo:lo + blk] for l1m in l1m_all],
                                              axis=0).astype(BF16)
        after_prev_all = jnp.dot(stack(0, n_prev), tri, preferred_element_type=F32)
        after_diag_all = jnp.dot(stack(blk, blk), tri, preferred_element_type=F32)
        carries, accs = [], []
        for hh in range(n_heads):
            l1m = l1m_all[hh]
            after_prev = after_prev_all[hh * n_prev:(hh + 1) * n_prev]
            after_diag = after_diag_all[hh * blk:(hh + 1) * blk]
            total_diag = after_diag[:, 0:1] + l1m[:, blk:blk + 1]
            after_p = after_prev + total_diag[:n_prev]
            a_rows, carry_rows = [], []
            for lsz, mask, (row0, col0) in zip(lsz_all[hh], masks, groups):
                rows = slice(row0, row0 + quarter)
                after = after_diag[rows] if col0 == blk else jnp.concatenate(
                    [after_p[rows, col0:], after_diag[rows]], axis=1)
                a = jnp.where(mask, jnp.exp2(lsz + after), 0.0)
                a_rows.append(pad_left(a, col0))
                if col0 < blk:
                    carry_rows.append(after_prev[rows, col0:col0 + 1] + total_diag[rows]
                                      + l1m[rows, col0:col0 + 1])
                else:
                    carry_rows.append(total_diag[rows])
            v_prev = v_ref[0, hh, prev]
            if sub == 0:
                v_prev = jnp.where(i > 0, v_prev, jnp.zeros((), BF16))
            a = jnp.concatenate(a_rows, axis=0).astype(BF16)
            acc = jnp.dot(a[:, blk:], v_ref[0, hh, i], preferred_element_type=F32)
            acc_prev = jnp.dot(a[:n_prev, :blk], v_prev, preferred_element_type=F32)
            accs.append(jnp.concatenate([acc[:n_prev] + acc_prev, acc[n_prev:]], axis=0))
            carries.append(carry_rows)
        o_ref[0, sub * blk:(sub + 1) * blk, :] = jnp.concatenate(accs, axis=-1).astype(BF16)
        return i, carries, accs

    def any_alive(carries):
        return jnp.max(functools.reduce(jnp.maximum, carries)) > DEAD_LOG2

    def earlier_blocks(sub, i, carries, accs):
        late_alive = jnp.logical_and(
            i > 0, any_alive([jnp.concatenate(c[1:], axis=1) for c in carries]))
        early_alive = jnp.logical_and(i > 1, any_alive([c[0] for c in carries]))

        @pl.when(jnp.logical_or(late_alive, early_alive))
        def _():
            for hh in range(n_heads):
                for (row0, _), carry in zip(groups, carries[hh]):
                    carry_ref[hh, row0:row0 + quarter] = carry
                acc_ref[hh] = accs[hh]

            tri_half = strict_lower(half)
            for rows, key_rows in ((slice(half + quarter, blk), slice(half, blk)),
                                   (slice(quarter, blk), slice(0, half))):
                for hh in range(n_heads):
                    lsz, l1m = log_terms(scores(q_ref[0, hh, sub, rows, :],
                                                k_ref[0, hh, i - 1, key_rows, :]))
                    after = jnp.dot(l1m.astype(BF16), tri_half, preferred_element_type=F32)
                    carry = carry_ref[hh, rows]
                    a = jnp.exp2(lsz + after + carry)
                    acc_ref[hh, rows] += jnp.dot(a.astype(BF16), v_ref[0, hh, i - 1, key_rows, :],
                                                 preferred_element_type=F32)
                    carry_ref[hh, rows] = carry + after[:, 0:1] + l1m[:, 0:1]

            def step(loop):
                it, _ = loop
                j = i - 2 - it
                new = []
                for hh in range(n_heads):
                    lsz, l1m = log_terms(scores(q_ref[0, hh, sub], k_ref[0, hh, j]))
                    after = jnp.dot(l1m.astype(BF16), tri, preferred_element_type=F32)
                    a = jnp.exp2(lsz + after + carry_ref[hh])
                    acc_ref[hh] += jnp.dot(a.astype(BF16), v_ref[0, hh, j],
                                           preferred_element_type=F32)
                    new.append(carry_ref[hh] + after[:, 0:1] + l1m[:, 0:1])
                    carry_ref[hh] = new[-1]
                return it + 1, any_alive(new)

            lax.while_loop(lambda loop: jnp.logical_and(loop[0] < i - 1, loop[1]),
                           step, (jnp.int32(0), True))
            o_ref[0, sub * blk:(sub + 1) * blk, :] = jnp.concatenate(
                [acc_ref[hh] for hh in range(n_heads)], axis=-1).astype(BF16)

    firsts = [first_tile(sub) for sub in range(n_sub)]
    for sub, (i, carries, accs) in enumerate(firsts):
        earlier_blocks(sub, i, carries, accs)


def _out_kernel(x_ref, osb_ref, gy_ref, gsb_ref, w_brs_ref, w_out_ref, g_post_mix_ref,
                g_pre_mlp_ref, w_up_ref, w_down_ref, g_post_mlp_ref, o_ref):
    def mixed(r):
        y_sb = jnp.dot(osb_ref[0, r], w_brs_ref[...], preferred_element_type=F32)
        merged = gy_ref[0, r].astype(F32) + gsb_ref[0, r].astype(F32) * y_sb
        return jnp.dot(merged.astype(BF16), w_out_ref[...], preferred_element_type=F32)

    def normed(r, mix):
        x1 = x_ref[0, r] + mix * _rms_scale(mix) * g_post_mix_ref[...]
        return x1, (x1 * _rms_scale(x1) * g_pre_mlp_ref[...]).astype(BF16)

    def mlp(h2):
        ff = None
        for c0 in range(0, w_up_ref.shape[1], FF_CHUNK):
            up = jnp.dot(h2, w_up_ref[:, c0:c0 + FF_CHUNK], preferred_element_type=F32)
            act = jnp.square(jnp.maximum(up, 0.0)).astype(BF16)
            part = jnp.dot(act, w_down_ref[c0:c0 + FF_CHUNK, :], preferred_element_type=F32)
            ff = part if ff is None else ff + part
        return ff

    half = x_ref.shape[1] // 2
    halves = (pl.ds(0, half), pl.ds(half, half))
    mixes = [mixed(r) for r in halves]
    normeds = [normed(r, mix) for r, mix in zip(halves, mixes)]
    for r, (x1, h2) in zip(halves, normeds):
        ff = mlp(h2)
        o_ref[0, r] = x1 + ff * _rms_scale(ff) * g_post_mlp_ref[...]


def _as_row(a):
    return a.reshape(1, -1).astype(F32)


def _tok_spec(width):
    return pl.BlockSpec((1, TOK_TILE, width), lambda b, s: (b, s, 0))


def _project(x, g_pre_mix, w_in, w_pool_mix, pool_scale, w_br_pool, w_gate, b_gate, n_heads,
             later_weights):
    B, S, D = x.shape
    sub = TOK_TILE // ATT_BLK
    n_tiles = S // TOK_TILE
    n_steps = B * n_tiles
    assert all(w.ndim == 2 and w.shape[0] % (BF16_ROWS * n_steps) == 0 for w in later_weights)
    slice_specs = [pl.BlockSpec((w.shape[0] // n_steps, w.shape[1]),
                                lambda b, s: (b * n_tiles + s, 0)) for w in later_weights]
    head_tile = pl.BlockSpec((1, n_heads, sub, ATT_BLK, HEAD_DIM), lambda b, s: (b, 0, s, 0, 0))
    head_shape = jax.ShapeDtypeStruct((B, n_heads, S // ATT_BLK, ATT_BLK, HEAD_DIM), BF16)
    return pl.pallas_call(
        functools.partial(_proj_kernel, n_heads=n_heads, d_model=D,
                          n_cast=len(later_weights)),
        grid=(B, n_tiles),
        in_specs=[
            _tok_spec(D),
            _const_spec((1, D)),
            _const_spec(w_in.shape),
            _const_spec(w_pool_mix.shape),
            _const_spec((1, POOL_WIDTH)),
            _const_spec(w_br_pool.shape),
            _const_spec(w_gate.shape),
            _const_spec((1, 2 * D)),
            *slice_specs,
        ],
        out_specs=[head_tile, head_tile, head_tile, _tok_spec(D), _tok_spec(D), *slice_specs],
        out_shape=[head_shape, head_shape, head_shape,
                   jax.ShapeDtypeStruct((B, S, D), BF16),
                   jax.ShapeDtypeStruct((B, S, D), BF16),
                   *[jax.ShapeDtypeStruct(w.shape, BF16) for w in later_weights]],
        scratch_shapes=[pltpu.VMEM((HALO + TOK_TILE, POOL_WIDTH), F32),
                        pltpu.VMEM((len(POOL_WINDOWS), HALO + TOK_TILE, POOL_GROUP), F32),
                        pltpu.VMEM(w_in.shape, BF16), pltpu.VMEM(w_pool_mix.shape, BF16),
                        pltpu.VMEM(w_br_pool.shape, BF16), pltpu.VMEM(w_gate.shape, BF16)],
        compiler_params=pltpu.CompilerParams(
            dimension_semantics=("arbitrary", "arbitrary"),
            vmem_limit_bytes=PROJ_VMEM_LIMIT_BYTES),
        name="proj_pool_gates",
    )(x, _as_row(g_pre_mix), w_in, w_pool_mix, _as_row(pool_scale), w_br_pool, w_gate,
      _as_row(b_gate), *later_weights)


def _attention(q, k, v):
    B, n_heads, n_blk = q.shape[:3]
    assert n_blk % ATT_QBLKS == 0
    q_spec = pl.BlockSpec((1, n_heads, ATT_QBLKS, ATT_BLK, HEAD_DIM),
                          lambda b, i: (b, 0, i, 0, 0))
    seq_spec = pl.BlockSpec((1, n_heads, n_blk, ATT_BLK, HEAD_DIM), lambda b, i: (b, 0, 0, 0, 0))
    sb_width = n_heads * HEAD_DIM
    return pl.pallas_call(
        _attn_kernel,
        grid=(B, n_blk // ATT_QBLKS),
        in_specs=[q_spec, seq_spec, seq_spec],
        out_specs=pl.BlockSpec((1, ATT_QBLKS * ATT_BLK, sb_width), lambda b, i: (b, i, 0)),
        out_shape=jax.ShapeDtypeStruct((B, n_blk * ATT_BLK, sb_width), BF16),
        scratch_shapes=[pltpu.VMEM((n_heads, ATT_BLK, 1), F32),
                        pltpu.VMEM((n_heads, ATT_BLK, HEAD_DIM), F32)],
        compiler_params=pltpu.CompilerParams(
            dimension_semantics=("parallel", "parallel"),
            vmem_limit_bytes=VMEM_LIMIT_BYTES),
        name="stickbreak_attn",
    )(q, k, v)


def _merge_mlp(x, o_sb, gy, gsb, w_br_sb, w_out, g_post_mix, g_pre_mlp, w_up, w_down,
               g_post_mlp):
    B, S, D = x.shape
    return pl.pallas_call(
        _out_kernel,
        grid=(B, S // TOK_TILE),
        in_specs=[
            _tok_spec(D),
            _tok_spec(o_sb.shape[-1]),
            _tok_spec(D),
            _tok_spec(D),
            _const_spec(w_br_sb.shape),
            _const_spec(w_out.shape),
            _const_spec((1, D)),
            _const_spec((1, D)),
            _const_spec(w_up.shape),
            _const_spec(w_down.shape),
            _const_spec((1, D)),
        ],
        out_specs=_tok_spec(D),
        out_shape=jax.ShapeDtypeStruct((B, S, D), x.dtype),
        compiler_params=pltpu.CompilerParams(
            dimension_semantics=("parallel", "parallel"),
            vmem_limit_bytes=VMEM_LIMIT_BYTES),
        name="merge_mlp",
    )(x, o_sb, gy, gsb, w_br_sb, w_out, _as_row(g_post_mix), _as_row(g_pre_mlp), w_up, w_down,
      _as_row(g_post_mlp))


def kernel(x, g_pre_mix, w_in, w_pool_mix, pool_scale, w_br_pool, w_br_sb, w_gate, b_gate,
           w_out, g_post_mix, g_pre_mlp, w_up, w_down, g_post_mlp):
    B, S, D = x.shape
    sb_width = w_br_sb.shape[0]
    assert w_in.shape[1] == POOL_WIDTH + 3 * sb_width and w_br_pool.shape[0] == POOL_WIDTH
    assert S % TOK_TILE == 0 and TOK_TILE % ATT_BLK == 0 and w_gate.shape[1] == 2 * D
    assert w_up.shape[1] % FF_CHUNK == 0

    q, k, v, gy, gsb, w_br_sb, w_out, w_up, w_down = _project(
        x, g_pre_mix, w_in, w_pool_mix, pool_scale, w_br_pool, w_gate, b_gate,
        sb_width // HEAD_DIM, later_weights=(w_br_sb, w_out, w_up, w_down))
    o_sb = _attention(q, k, v)
    return _merge_mlp(x, o_sb, gy, gsb, w_br_sb, w_out, g_post_mix, g_pre_mlp, w_up, w_down,
                      g_post_mlp)
```
